```python
import math
import functools
import jax
import jax.numpy as jnp
from jax import lax
import numpy as np

D_MODEL = 2048
BATCH = 4
SEQ = 2048
DEPTH = 2

GRID_W = 64
CTX_LEN = 256
N_BRANCH = 4
BRANCH_W = 1024
EPS = 1e-6
ROPE_BASE = 10000.0

SSD_HEADS = 16
SSD_HEAD_DIM = 64
SSD_GROUPS = 4
SSD_STATE = 128
SSD_CONV = 5
SSD_CHUNK = 128

SWA_HEADS = 16
SWA_KV_HEADS = 4
SWA_HEAD_DIM = 64
WINDOW = 128

HG_HEADS = 8
HG_KEY_DIM = 128
HG_VAL_DIM = 128
HG_CHUNK = 64

MLA_HEADS = 8
MLA_Q_RANK = 512
MLA_KV_RANK = 256
MLA_NOPE = 128
MLA_ROPE = 64
MLA_V = 128
Q_BLOCK = 128

N_EXPERTS = 64
TOP_K = 6
N_EXPERT_GROUPS = 8
TOPK_EXPERT_GROUPS = 4
EXPERT_FF = 512
SHARED_FF = 1024
ROUTE_SCALE = 2.5
MOE_BLOCK = 128

IN_LAYOUT = (
    ('ssd_z', SSD_HEADS * SSD_HEAD_DIM),
    ('ssd_x', SSD_HEADS * SSD_HEAD_DIM),
    ('ssd_B', SSD_GROUPS * SSD_STATE),
    ('ssd_C', SSD_GROUPS * SSD_STATE),
    ('ssd_dt', 2 * SSD_HEADS),
    ('swa_q', SWA_HEADS * SWA_HEAD_DIM),
    ('swa_k', SWA_KV_HEADS * SWA_HEAD_DIM),
    ('swa_v', SWA_KV_HEADS * SWA_HEAD_DIM),
    ('hg_q', HG_HEADS * HG_KEY_DIM),
    ('hg_f', 2 * HG_HEADS * HG_KEY_DIM),
    ('hg_i', HG_HEADS * HG_VAL_DIM),
    ('hg_g', HG_HEADS * HG_VAL_DIM),
    ('mla_cq', MLA_Q_RANK),
    ('mla_ckv', MLA_KV_RANK),
    ('mla_kr', MLA_ROPE),
    ('gates', N_BRANCH * D_MODEL),
)

kernel_name = 'hybrid_ssd_swa_hgrn2_mla_moe_dit'


def rmsnorm(x, gain):
    xf = x.astype(jnp.float32)
    y = xf * lax.rsqrt(jnp.mean(xf * xf, axis=-1, keepdims=True) + EPS)
    return (y * gain.astype(jnp.float32)).astype(x.dtype)


def modulation(cond, w_mod, b_mod):
    m = jnp.einsum('...d,de->...e', jax.nn.silu(cond), w_mod) + b_mod
    return jnp.split(m[..., None, :], 6, axis=-1)


def in_project(h, w_in):
    p = jnp.einsum('bld,de->ble', h, w_in)
    parts, start = {}, 0
    for name, size in IN_LAYOUT:
        parts[name] = p[..., start:start + size]
        start += size
    return parts


def axial_rope_tables(row, col, rot_dim):
    n_freq = rot_dim // 4
    inv_freq = ROPE_BASE ** (-jnp.arange(n_freq, dtype=jnp.float32) / n_freq)
    ang = jnp.concatenate([row.astype(jnp.float32)[:, None] * inv_freq,
                           col.astype(jnp.float32)[:, None] * inv_freq], axis=-1)
    return jnp.cos(ang), jnp.sin(ang)


def apply_rope(t, cos, sin):
    half = t.shape[-1] // 2
    t1 = t[..., :half].astype(jnp.float32)
    t2 = t[..., half:].astype(jnp.float32)
    c, s = cos[:, None, :], sin[:, None, :]
    return jnp.concatenate([t1 * c - t2 * s, t1 * s + t2 * c], axis=-1).astype(t.dtype)


def centred_dwconv(u, w, bias):
    n_ch = u.shape[-1]
    half = w.shape[0] // 2
    y = lax.conv_general_dilated(u, w[:, None, :].astype(u.dtype), window_strides=(1,), padding=[(half, half)],
                                 dimension_numbers=('NWC', 'WIO', 'NWC'), feature_group_count=n_ch)
    return y + bias


def ssd_scan(x, dt, B, C, s0, A):
    b, L, h, p = x.shape
    g, n = B.shape[2], B.shape[3]
    r = h // g
    Q = SSD_CHUNK
    nc = L // Q
    f32 = jnp.float32
    xq = x.astype(f32).reshape(b, nc, Q, g, r, p)
    dtq = dt.astype(f32).reshape(b, nc, Q, g, r)
    Bq = B.astype(f32).reshape(b, nc, Q, g, n)
    Cq = C.astype(f32).reshape(b, nc, Q, g, n)
    a_cum = jnp.cumsum(dtq * A.reshape(g, r), axis=2)
    lower = jnp.tril(jnp.ones((Q, Q), bool))[:, :, None, None]
    decay = jnp.exp(jnp.where(lower, a_cum[:, :, :, None] - a_cum[:, :, None, :], -jnp.inf))
    cb = jnp.einsum('bcign,bcjgn->bcijg', Cq, Bq)
    dtx = dtq[..., None] * xq
    y_diag = jnp.einsum('bcijgr,bcjgrp->bcigrp', cb[..., None] * decay, dtx)
    a_last = a_cum[:, :, -1]
    w_state = jnp.exp(a_last[:, :, None] - a_cum)[..., None] * dtx
    chunk_states = jnp.einsum('bcjgrp,bcjgn->bcgrpn', w_state, Bq)

    def carry_state(s, inp):
        st, al = inp
        return s * jnp.exp(al)[..., None, None] + st, s

    s_fin, s_start = lax.scan(carry_state, s0.astype(f32).reshape(b, g, r, p, n),
                              (jnp.moveaxis(chunk_states, 1, 0), jnp.moveaxis(a_last, 1, 0)))
    s_start = jnp.moveaxis(s_start, 0, 1)
    y_off = jnp.einsum('bcign,bcgrpn->bcigrp', Cq, s_start) * jnp.exp(a_cum)[..., None]
    y = (y_diag + y_off).reshape(b, L, h, p).astype(x.dtype)
    return y, s_fin.reshape(b, h, p, n)


def hgrn_scan(q, k, logf, v, s0):
    b, L, h, dk = q.shape
    dv = v.shape[-1]
    Q = HG_CHUNK
    nc = L // Q

    def chunks(t):
        return jnp.moveaxis(t.astype(jnp.float32).reshape(b, nc, Q, h, t.shape[-1]), 1, 0)

    lower = jnp.tril(jnp.ones((Q, Q), bool))[:, :, None, None]

    def step(s, inp):
        qc, kc, fc, vc = inp
        cum = jnp.cumsum(fc, axis=1)
        o_inter = jnp.einsum('bihk,bhkv->bihv', qc * jnp.exp(cum), s)
        decay = jnp.exp(jnp.where(lower, cum[:, :, None] - cum[:, None, :], -jnp.inf))
        att = jnp.einsum('bijhk,bjhk->bhij', qc[:, :, None] * decay, kc)
        o_intra = jnp.einsum('bhij,bjhv->bihv', att, vc)
        last = cum[:, -1]
        s_new = s * jnp.exp(last)[..., None] + jnp.einsum('bjhk,bjhv->bhkv', kc * jnp.exp(last[:, None] - cum), vc)
        return s_new, o_inter + o_intra

    s_fin, o = lax.scan(step, s0.astype(jnp.float32), (chunks(q), chunks(k), chunks(logf), chunks(v)))
    return jnp.moveaxis(o, 0, 1).reshape(b, L, h, dv).astype(v.dtype), s_fin


def two_stream_scan(scan_fn, ctx_seqs, lat_seqs, s0, reverse):
    flip = (lambda t: jnp.flip(t, axis=1)) if reverse else (lambda t: t)
    y_ctx, s_ctx = scan_fn(*[flip(t) for t in ctx_seqs], s0)
    y_lat, _ = scan_fn(*[flip(t) for t in lat_seqs], s_ctx)
    return flip(y_ctx), flip(y_lat)


def ssd_mixer(pc, pl, conv_w, conv_b, dt_bias, a_log, d_skip, norm_g, need_ctx):
    d_in = SSD_HEADS * SSD_HEAD_DIM
    bc_w = SSD_GROUPS * SSD_STATE

    def prep(p):
        b, L = p['ssd_x'].shape[:2]
        xbc = jnp.concatenate([p['ssd_x'], p['ssd_B'], p['ssd_C']], axis=-1)
        xbc = jax.nn.silu(centred_dwconv(xbc, conv_w, conv_b))
        xs = xbc[..., :d_in].reshape(b, L, SSD_HEADS, SSD_HEAD_DIM)
        Bm = xbc[..., d_in:d_in + bc_w].reshape(b, L, SSD_GROUPS, SSD_STATE)
        Cm = xbc[..., d_in + bc_w:].reshape(b, L, SSD_GROUPS, SSD_STATE)
        dt = jax.nn.softplus(p['ssd_dt'].astype(jnp.float32).reshape(b, L, 2, SSD_HEADS) + dt_bias.astype(jnp.float32))
        return xs, Bm, Cm, dt

    xs_c, B_c, C_c, dt_c = prep(pc)
    xs_l, B_l, C_l, dt_l = prep(pl)
    s0 = jnp.zeros((xs_l.shape[0], SSD_HEADS, SSD_HEAD_DIM, SSD_STATE), jnp.float32)
    outs = []
    for d in range(2):
        scan = functools.partial(ssd_scan, A=-jnp.exp(a_log[d].astype(jnp.float32)))
        outs.append(two_stream_scan(scan, (xs_c, dt_c[:, :, d], B_c, C_c), (xs_l, dt_l[:, :, d], B_l, C_l), s0, d == 1))

    def finish(y, xs, z):
        b, L = y.shape[:2]
        y = (y + xs * d_skip[:, None].astype(xs.dtype)).reshape(b, L, d_in)
        return rmsnorm(y * jax.nn.silu(z), norm_g)

    y_lat = finish(outs[0][1] + outs[1][1], xs_l, pl['ssd_z'])
    y_ctx = finish(outs[0][0] + outs[1][0], xs_c, pc['ssd_z']) if need_ctx else None
    return y_ctx, y_lat


def hgrn_mixer(pc, pl, lb, norm_g, need_ctx):
    lb = lb.reshape(2, HG_HEADS, HG_KEY_DIM).astype(jnp.float32)

    def prep(p):
        b, L = p['hg_q'].shape[:2]
        q = jax.nn.silu(p['hg_q']).reshape(b, L, HG_HEADS, HG_KEY_DIM)
        f = lb + (1.0 - lb) * jax.nn.sigmoid(p['hg_f'].astype(jnp.float32).reshape(b, L, 2, HG_HEADS, HG_KEY_DIM))
        v = p['hg_i'].reshape(b, L, HG_HEADS, HG_VAL_DIM)
        return q, f, v

    q_c, f_c, v_c = prep(pc)
    q_l, f_l, v_l = prep(pl)
    s0 = jnp.zeros((q_l.shape[0], HG_HEADS, HG_KEY_DIM, HG_VAL_DIM), jnp.float32)
    outs = [two_stream_scan(hgrn_scan,
                            (q_c, 1.0 - f_c[:, :, d], jnp.log(f_c[:, :, d]), v_c),
                            (q_l, 1.0 - f_l[:, :, d], jnp.log(f_l[:, :, d]), v_l), s0, d == 1)
            for d in range(2)]

    def finish(y, p):
        b, L = y.shape[:2]
        g = jax.nn.sigmoid(p['hg_g']).reshape(b, L, HG_HEADS, HG_VAL_DIM)
        return (rmsnorm(y, norm_g) * g).reshape(b, L, HG_HEADS * HG_VAL_DIM)

    y_lat = finish(outs[0][1] + outs[1][1], pl)
    y_ctx = finish(outs[0][0] + outs[1][0], pc) if need_ctx else None
    return y_ctx, y_lat


def grouped_attention(q, k, v, sink, scale):
    n_k = k.shape[1]
    s = jnp.einsum('bqgrd,bkgd->bgrqk', q, k).astype(jnp.float32) * scale
    if sink is not None:
        sink_col = jnp.broadcast_to(sink.astype(jnp.float32)[None, :, :, None, None], s.shape[:-1] + (1,))
        s = jnp.concatenate([s, sink_col], axis=-1)
    p = jax.nn.softmax(s, axis=-1)[..., :n_k]
    return jnp.einsum('bgrqk,bkgd->bqgrd', p.astype(v.dtype), v)


def blocked_attention(q, k, v, scale):
    b, L = q.shape[:2]
    nb = L // Q_BLOCK
    qb = jnp.moveaxis(q.reshape((b, nb, Q_BLOCK) + q.shape[2:]), 1, 0)
    o = lax.map(lambda qq: grouped_attention(qq, k, v, None, scale), qb)
    return jnp.moveaxis(o, 0, 1).reshape((b, L) + o.shape[3:])


def banded_window_attention(q, k, v, k_ctx, v_ctx, sink, scale):
    b, L, G, R, d = q.shape
    W = WINDOW
    nb = L // W
    qb = q.reshape(b, nb, W, G, R, d)

    def band(t):
        pad = jnp.zeros((b, W) + t.shape[2:], t.dtype)
        tp = jnp.concatenate([pad, t, pad], axis=1).reshape((b, nb + 2, W) + t.shape[2:])
        return jnp.concatenate([tp[:, :-2], tp[:, 1:-1], tp[:, 2:]], axis=2)

    kb, vb = band(k), band(v)
    s_lat = jnp.einsum('bnqgrd,bnkgd->bngrqk', qb, kb).astype(jnp.float32) * scale
    q_off = jnp.arange(W)[:, None]
    k_off = jnp.arange(3 * W)[None, :] - W
    k_abs = jnp.arange(nb)[:, None, None] * W + k_off
    valid = (jnp.abs(k_off - q_off) <= W) & (k_abs >= 0) & (k_abs < L)
    s_lat = jnp.where(valid[None, :, None, None], s_lat, -jnp.inf)
    s_ctx = jnp.einsum('bnqgrd,bcgd->bngrqc', qb, k_ctx).astype(jnp.float32) * scale
    sink_col = jnp.broadcast_to(sink.astype(jnp.float32)[None, None, :, :, None, None], s_lat.shape[:-1] + (1,))
    p = jax.nn.softmax(jnp.concatenate([s_lat, s_ctx, sink_col], axis=-1), axis=-1)
    n_band, n_ctx = 3 * W, k_ctx.shape[1]
    o = (jnp.einsum('bngrqk,bnkgd->bnqgrd', p[..., :n_band].astype(v.dtype), vb)
         + jnp.einsum('bngrqc,bcgd->bnqgrd', p[..., n_band:n_band + n_ctx].astype(v.dtype), v_ctx))
    return o.reshape(b, L, G, R, d)


def swa_mixer(pc, pl, sink, cos, sin, need_ctx):
    G, R, d = SWA_KV_HEADS, SWA_HEADS // SWA_KV_HEADS, SWA_HEAD_DIM
    scale = d ** -0.5

    def heads(p):
        b, L = p['swa_q'].shape[:2]
        return (p['swa_q'].reshape(b, L, SWA_HEADS, d), p['swa_k'].reshape(b, L, G, d), p['swa_v'].reshape(b, L, G, d))

    q_l, k_l, v_l = heads(pl)
    q_c, k_c, v_c = heads(pc)
    q_l = apply_rope(q_l, cos, sin)
    k_l = apply_rope(k_l, cos, sin)
    b, L = q_l.shape[:2]
    sink_gr = sink.reshape(G, R)
    y_lat = banded_window_attention(q_l.reshape(b, L, G, R, d), k_l, v_l, k_c, v_c, sink_gr, scale).reshape(b, L, G * R * d)
    y_ctx = None
    if need_ctx:
        Lc = q_c.shape[1]
        y_ctx = grouped_attention(q_c.reshape(b, Lc, G, R, d), k_c, v_c, sink_gr, scale).reshape(b, Lc, G * R * d)
    return y_ctx, y_lat


def mla_mixer(pc, pl, q_norm_g, kv_norm_g, w_uq, w_ukv, cos, sin, need_ctx):
    H = MLA_HEADS
    scale = (MLA_NOPE + MLA_ROPE) ** -0.5

    def heads(p, rotate):
        b, L = p['mla_cq'].shape[:2]
        q = jnp.einsum('blr,re->ble', rmsnorm(p['mla_cq'], q_norm_g), w_uq).reshape(b, L, H, MLA_NOPE + MLA_ROPE)
        kv = jnp.einsum('blr,re->ble', rmsnorm(p['mla_ckv'], kv_norm_g), w_ukv).reshape(b, L, H, MLA_NOPE + MLA_V)
        q_nope, q_rope = q[..., :MLA_NOPE], q[..., MLA_NOPE:]
        k_nope, v = kv[..., :MLA_NOPE], kv[..., MLA_NOPE:]
        k_rope = p['mla_kr'].reshape(b, L, 1, MLA_ROPE)
        if rotate:
            q_rope = apply_rope(q_rope, cos, sin)
            k_rope = apply_rope(k_rope, cos, sin)
        q = jnp.concatenate([q_nope, q_rope], axis=-1)
        k = jnp.concatenate([k_nope, jnp.broadcast_to(k_rope, (b, L, H, MLA_ROPE))], axis=-1)
        return q, k, v

    q_l, k_l, v_l = heads(pl, True)
    q_c, k_c, v_c = heads(pc, False)
    k_all = jnp.concatenate([k_c, k_l], axis=1)
    v_all = jnp.concatenate([v_c, v_l], axis=1)
    b, L = q_l.shape[:2]
    y_lat = blocked_attention(q_l[:, :, :, None], k_all, v_all, scale).reshape(b, L, H * MLA_V)
    y_ctx = None
    if need_ctx:
        Lc = q_c.shape[1]
        y_ctx = grouped_attention(q_c[:, :, :, None], k_c, v_c, None, scale).reshape(b, Lc, H * MLA_V)
    return y_ctx, y_lat


def swiglu(t, w_gate, w_up, w_down):
    return (jax.nn.silu(t @ w_gate) * (t @ w_up)) @ w_down


def grouped_expert_ffn(t, top_e, top_w, w_gate, w_up, w_down):
    T, d = t.shape
    n_assign = T * TOP_K
    flat_e = top_e.reshape(n_assign)
    order = jnp.argsort(flat_e)
    e_sorted = flat_e[order]
    counts = jnp.bincount(flat_e, length=N_EXPERTS)
    padded = (counts + MOE_BLOCK - 1) // MOE_BLOCK * MOE_BLOCK
    pad_end = jnp.cumsum(padded)
    pad_start = pad_end - padded
    grp_start = jnp.cumsum(counts) - counts
    dest = pad_start[e_sorted] + jnp.arange(n_assign) - grp_start[e_sorted]
    n_blocks = -(-n_assign // MOE_BLOCK) + N_EXPERTS
    n_rows = n_blocks * MOE_BLOCK
    row_tok = jnp.full((n_rows,), T, jnp.int32).at[dest].set((order // TOP_K).astype(jnp.int32))
    row_w = jnp.zeros((n_rows,), jnp.float32).at[dest].set(top_w.reshape(n_assign)[order])
    blk_e = jnp.minimum(jnp.searchsorted(pad_end, jnp.arange(n_blocks) * MOE_BLOCK, side='right'), N_EXPERTS - 1)
    t_pad = jnp.concatenate([t, jnp.zeros((1, d), t.dtype)], axis=0)

    def block(args):
        toks, wts, e = args
        xb = t_pad[toks]
        hb = jax.nn.silu(xb @ w_gate[e]) * (xb @ w_up[e])
        return (hb @ w_down[e]).astype(jnp.float32) * wts[:, None]

    y = lax.map(block, (row_tok.reshape(n_blocks, MOE_BLOCK), row_w.reshape(n_blocks, MOE_BLOCK), blk_e))
    out = jax.ops.segment_sum(y.reshape(n_rows, d), row_tok, num_segments=T + 1)
    return out[:T].astype(t.dtype)


def moe_ffn(h, router_w, router_bias, e_gate, e_up, e_down, s_gate, s_up, s_down):
    b, L, d = h.shape
    t = h.reshape(b * L, d)
    T = t.shape[0]
    scores = jax.nn.sigmoid(jnp.einsum('td,de->te', t, router_w).astype(jnp.float32))
    sel = scores + router_bias.astype(jnp.float32)
    per_group = N_EXPERTS // N_EXPERT_GROUPS
    group_score = lax.top_k(sel.reshape(T, N_EXPERT_GROUPS, per_group), 2)[0].sum(-1)
    _, top_groups = lax.top_k(group_score, TOPK_EXPERT_GROUPS)
    group_mask = jax.nn.one_hot(top_groups, N_EXPERT_GROUPS, dtype=jnp.float32).sum(1) > 0
    sel = jnp.where(jnp.repeat(group_mask, per_group, axis=1), sel, -jnp.inf)
    _, top_e = lax.top_k(sel, TOP_K)
    top_w = jnp.take_along_axis(scores, top_e, axis=1)
    top_w = top_w / jnp.sum(top_w, axis=-1, keepdims=True) * ROUTE_SCALE
    routed = grouped_expert_ffn(t, top_e, top_w, e_gate, e_up, e_down)
    return (routed + swiglu(t, s_gate, s_up, s_down)).reshape(b, L, d)


def hybrid_layer(xl, xc, c, c_ctx, prm, lb, rope_swa, rope_mla, need_ctx):
    shift1, scale1, gate1, shift2, scale2, gate2 = modulation(c, prm['w_mod'], prm['b_mod'])
    cshift1, cscale1, cgate1, cshift2, cscale2, cgate2 = modulation(c_ctx, prm['w_mod'], prm['b_mod'])
    h_l = rmsnorm(xl, prm['norm_pre_mix']) * (1.0 + scale1) + shift1
    h_c = rmsnorm(xc, prm['norm_pre_mix']) * (1.0 + cscale1) + cshift1
    pl = in_project(h_l, prm['w_in'])
    pc = in_project(h_c, prm['w_in'])
    ssd_c, ssd_l = ssd_mixer(pc, pl, prm['ssd_conv_w'], prm['ssd_conv_b'], prm['ssd_dt_bias'], prm['ssd_a_log'],
                             prm['ssd_d'], prm['ssd_norm'], need_ctx)
    swa_c, swa_l = swa_mixer(pc, pl, prm['swa_sink'], rope_swa[0], rope_swa[1], need_ctx)
    hg_c, hg_l = hgrn_mixer(pc, pl, lb, prm['hg_norm'], need_ctx)
    mla_c, mla_l = mla_mixer(pc, pl, prm['mla_q_norm'], prm['mla_kv_norm'], prm['mla_w_uq'], prm['mla_w_ukv'],
                             rope_mla[0], rope_mla[1], need_ctx)

    def merge_out(branches, gates):
        y = jnp.stack(branches, axis=2)
        b, L = y.shape[:2]
        g = jax.nn.sigmoid(gates.reshape(b, L, N_BRANCH, D_MODEL))
        m = jnp.sum(g * jnp.einsum('blnw,nwd->blnd', y, prm['w_branch']), axis=2)
        return jnp.einsum('bld,de->ble', m, prm['w_out'])

    xl = xl + gate1 * rmsnorm(merge_out([ssd_l, swa_l, hg_l, mla_l], pl['gates']), prm['norm_post_mix'])
    h2_l = rmsnorm(xl, prm['norm_pre_ffn']) * (1.0 + scale2) + shift2
    moe_w = (prm['router_w'], prm['router_bias'], prm['expert_w_gate'], prm['expert_w_up'], prm['expert_w_down'],
             prm['shared_w_gate'], prm['shared_w_up'], prm['shared_w_down'])
    if need_ctx:
        xc = xc + cgate1 * rmsnorm(merge_out([ssd_c, swa_c, hg_c, mla_c], pc['gates']), prm['norm_post_mix'])
        h2_c = rmsnorm(xc, prm['norm_pre_ffn']) * (1.0 + cscale2) + cshift2
        n_ctx = xc.shape[1]
        f = moe_ffn(jnp.concatenate([h2_c, h2_l], axis=1), *moe_w)
        f_c, f_l = f[:, :n_ctx], f[:, n_ctx:]
        xc = xc + cgate2 * rmsnorm(f_c, prm['norm_post_ffn'])
    else:
        f_l = moe_ffn(h2_l, *moe_w)
    xl = xl + gate2 * rmsnorm(f_l, prm['norm_post_ffn'])
    return xl, xc


def setup_inputs(seed: int = 0) -> dict:
    key = jax.random.key(seed)
    ks = iter(jax.random.split(key, 48))
    f32 = jnp.float32

    def nrm(shape, scale):
        return jax.random.normal(next(ks), shape, f32) * scale

    def gain(shape):
        return 1.0 + nrm(shape, 0.02)

    L = DEPTH
    d_in = sum(size for _, size in IN_LAYOUT)
    n_xbc = SSD_HEADS * SSD_HEAD_DIM + 2 * SSD_GROUPS * SSD_STATE
    dt0 = jnp.exp(jax.random.uniform(next(ks), (L, 2, SSD_HEADS), f32, math.log(1e-3), math.log(1e-1)))
    return {
        'x': nrm((BATCH, SEQ, D_MODEL), 1.0),
        'c': nrm((BATCH, D_MODEL), 1.0),
        'ctx': nrm((BATCH, CTX_LEN, D_MODEL), 1.0),
        'c_ctx': nrm((D_MODEL,), 1.0),
        'w_mod': nrm((L, D_MODEL, 6 * D_MODEL), 0.5 * D_MODEL ** -0.5),
        'b_mod': nrm((L, 6 * D_MODEL), 0.02),
        'norm_pre_mix': gain((L, D_MODEL)),
        'norm_post_mix': gain((L, D_MODEL)),
        'norm_pre_ffn': gain((L, D_MODEL)),
        'norm_post_ffn': gain((L, D_MODEL)),
        'w_in': nrm((L, D_MODEL, d_in), D_MODEL ** -0.5),
        'ssd_conv_w': nrm((L, SSD_CONV, n_xbc), SSD_CONV ** -0.5),
        'ssd_conv_b': nrm((L, n_xbc), 0.02),
        'ssd_dt_bias': dt0 + jnp.log(-jnp.expm1(-dt0)),
        'ssd_a_log': jnp.log(jax.random.uniform(next(ks), (L, 2, SSD_HEADS), f32, 1.0, 16.0)),
        'ssd_d': gain((L, SSD_HEADS)),
        'ssd_norm': gain((L, SSD_HEADS * SSD_HEAD_DIM)),
        'swa_sink': nrm((L, SWA_HEADS), 0.5),
        'hg_lb_logits': nrm((L, 2, HG_HEADS * HG_KEY_DIM), 0.5),
        'hg_norm': gain((L, HG_HEADS, HG_VAL_DIM)),
        'mla_q_norm': gain((L, MLA_Q_RANK)),
        'mla_kv_norm': gain((L, MLA_KV_RANK)),
        'mla_w_uq': nrm((L, MLA_Q_RANK, MLA_HEADS * (MLA_NOPE + MLA_ROPE)), MLA_Q_RANK ** -0.5),
        'mla_w_ukv': nrm((L, MLA_KV_RANK, MLA_HEADS * (MLA_NOPE + MLA_V)), MLA_KV_RANK ** -0.5),
        'w_branch': nrm((L, N_BRANCH, BRANCH_W, D_MODEL), BRANCH_W ** -0.5),
        'w_out': nrm((L, D_MODEL, D_MODEL), D_MODEL ** -0.5),
        'router_w': nrm((L, D_MODEL, N_EXPERTS), D_MODEL ** -0.5),
        'router_bias': nrm((L, N_EXPERTS), 0.01),
        'expert_w_gate': nrm((L, N_EXPERTS, D_MODEL, EXPERT_FF), D_MODEL ** -0.5),
        'expert_w_up': nrm((L, N_EXPERTS, D_MODEL, EXPERT_FF), D_MODEL ** -0.5),
        'expert_w_down': nrm((L, N_EXPERTS, EXPERT_FF, D_MODEL), EXPERT_FF ** -0.5),
        'shared_w_gate': nrm((L, D_MODEL, SHARED_FF), D_MODEL ** -0.5),
        'shared_w_up': nrm((L, D_MODEL, SHARED_FF), D_MODEL ** -0.5),
        'shared_w_down': nrm((L, SHARED_FF, D_MODEL), SHARED_FF ** -0.5),
    }


def reference(x, c, ctx, c_ctx, w_mod, b_mod, norm_pre_mix, norm_post_mix, norm_pre_ffn, norm_post_ffn, w_in,
              ssd_conv_w, ssd_conv_b, ssd_dt_bias, ssd_a_log, ssd_d, ssd_norm, swa_sink, hg_lb_logits, hg_norm,
              mla_q_norm, mla_kv_norm, mla_w_uq, mla_w_ukv, w_branch, w_out, router_w, router_bias,
              expert_w_gate, expert_w_up, expert_w_down, shared_w_gate, shared_w_up, shared_w_down):
    seq = x.shape[1]
    rows = seq // GRID_W
    row = jnp.repeat(jnp.arange(rows), GRID_W)
    col = jnp.tile(jnp.arange(GRID_W), rows)
    rope_swa = axial_rope_tables(row, col, SWA_HEAD_DIM)
    rope_mla = axial_rope_tables(row, col, MLA_ROPE)
    sm = jax.nn.softmax(hg_lb_logits.astype(jnp.float32), axis=0)
    lower_bounds = jnp.cumsum(sm, axis=0) - sm[0]
    xl, xc = x, ctx
    for l in range(DEPTH):
        prm = {
            'w_mod': w_mod[l], 'b_mod': b_mod[l],
            'norm_pre_mix': norm_pre_mix[l], 'norm_post_mix': norm_post_mix[l],
            'norm_pre_ffn': norm_pre_ffn[l], 'norm_post_ffn': norm_post_ffn[l],
            'w_in': w_in[l],
            'ssd_conv_w': ssd_conv_w[l], 'ssd_conv_b': ssd_conv_b[l], 'ssd_dt_bias': ssd_dt_bias[l],
            'ssd_a_log': ssd_a_log[l], 'ssd_d': ssd_d[l], 'ssd_norm': ssd_norm[l],
            'swa_sink': swa_sink[l], 'hg_norm': hg_norm[l],
            'mla_q_norm': mla_q_norm[l], 'mla_kv_norm': mla_kv_norm[l],
            'mla_w_uq': mla_w_uq[l], 'mla_w_ukv': mla_w_ukv[l],
            'w_branch': w_branch[l], 'w_out': w_out[l],
            'router_w': router_w[l], 'router_bias': router_bias[l],
            'expert_w_gate': expert_w_gate[l], 'expert_w_up': expert_w_up[l], 'expert_w_down': expert_w_down[l],
            'shared_w_gate': shared_w_gate[l], 'shared_w_up': shared_w_up[l], 'shared_w_down': shared_w_down[l],
        }
        xl, xc = hybrid_layer(xl, xc, c, c_ctx, prm, lower_bounds[l], rope_swa, rope_mla, l < DEPTH - 1)
    return xl
```

```python
import functools
import math

import jax
import jax.numpy as jnp
import numpy as np
from jax import lax
from jax.experimental import pallas as pl
from jax.experimental.pallas import tpu as pltpu

f32 = jnp.float32
bf16 = jnp.bfloat16

D_MODEL = 2048
GRID_W = 64
N_BRANCH = 4
BRANCH_W = 1024
EPS = 1e-6
ROPE_BASE = 10000.0

SSD_HEADS = 16
SSD_HEAD_DIM = 64
SSD_GROUPS = 4
SSD_STATE = 128
SSD_CONV = 5
SSD_CHUNK = 128

SWA_HEADS = 16
SWA_KV_HEADS = 4
SWA_HEAD_DIM = 64
WINDOW = 128

HG_HEADS = 8
HG_KEY_DIM = 128
HG_VAL_DIM = 128
HG_CHUNK = 64
HG_DIAG = 8

MLA_HEADS = 8
MLA_Q_RANK = 512
MLA_KV_RANK = 256
MLA_NOPE = 128
MLA_ROPE = 64
MLA_V = 128

N_EXPERTS = 64
TOP_K = 6
N_EXPERT_GROUPS = 8
TOPK_EXPERT_GROUPS = 4
EXPERT_FF = 512
SHARED_FF = 1024
ROUTE_SCALE = 2.5
MOE_ROWS = 256

LANE = 128
ROW_TILE = 256
VMEM_LIMIT = 48 * 1024 * 1024

_IN_LAYOUT = (
    ('ssd_z', 1024), ('ssd_x', 1024), ('ssd_B', 512), ('ssd_C', 512), ('ssd_dt', 32),
    ('swa_q', 1024), ('swa_k', 256), ('swa_v', 256),
    ('hg_q', 1024), ('hg_f', 2048), ('hg_i', 1024), ('hg_g', 1024),
    ('mla_cq', 512), ('mla_ckv', 256), ('mla_kr', 64), ('gates', 8192),
)
_P_ORDER = ('gates', 'hg_f', 'ssd_x', 'ssd_B', 'ssd_C', 'ssd_z', 'swa_q', 'hg_q', 'hg_i', 'hg_g',
            'mla_cq', 'swa_k', 'swa_v', 'mla_ckv', 'mla_kr', 'ssd_dt')
_P_TN = 512


def _layout():
    src, off = {}, 0
    for name, size in _IN_LAYOUT:
        src[name] = (off, size)
        off += size
    dst, off = {}, 0
    for name in _P_ORDER:
        dst[name] = off
        off += src[name][1]
    width = -(-off // _P_TN) * _P_TN
    return src, dst, off, width


_SRC, _DST, _P_USED, _P_WIDTH = _layout()
_MISC = _DST['mla_kr']
_DT_LANE = _DST['ssd_dt'] - _MISC


def _cp(*sem):
    return pltpu.CompilerParams(dimension_semantics=sem, vmem_limit_bytes=VMEM_LIMIT)


def _sigmoid(x):
    return 1.0 / (1.0 + jnp.exp(-x))


def _silu(x):
    return x * _sigmoid(x)


def _rms(x, gain):
    return x * lax.rsqrt(jnp.mean(x * x, axis=-1, keepdims=True) + EPS) * gain


def _dot(a, b):
    return jnp.dot(a, b, preferred_element_type=f32)


def _dot_nt(a, b):
    return lax.dot_general(a, b, (((1,), (1,)), ((), ())), preferred_element_type=f32)


def _dot_tn(a, b):
    return lax.dot_general(a, b, (((0,), (0,)), ((), ())), preferred_element_type=f32)


def _dot_exact(a, b):
    return jnp.dot(a, b, preferred_element_type=f32, precision=lax.Precision.HIGHEST)


class _Rows:
    def __init__(self, B, LC, LL, lat_only, tm=ROW_TILE):
        self.B, self.LC, self.LL, self.tm = B, LC, LL, tm
        self.nC, self.nL = LC // tm, LL // tm
        self.nS = self.nC + self.nL
        self.lat_only = lat_only
        self.n = B * (self.nL if lat_only else self.nS)

    def src(self, i):
        if self.lat_only:
            return (i // self.nL) * self.nS + self.nC + i % self.nL
        return i

    def cond(self, i):
        if self.lat_only:
            return i // self.nL
        return jnp.where(i % self.nS < self.nC, self.B, i // self.nS)

    def pos(self, i):
        if self.lat_only:
            return self.nC + i % self.nL
        return i % self.nS


def _mod_kernel(c_ref, w_ref, b_ref, o_ref):
    s = _silu(c_ref[...]).astype(bf16)
    o_ref[...] = _dot(s, w_ref[...].astype(bf16)) + b_ref[...]


def _modulation(cond, w_mod, b_mod):
    nl, d, n = w_mod.shape
    nc = cond.shape[0]
    tn = 1024
    return pl.pallas_call(
        _mod_kernel, grid=(nl, n // tn),
        in_specs=[pl.BlockSpec((nc, d), lambda l, j: (0, 0)),
                  pl.BlockSpec((None, d, tn), lambda l, j: (l, 0, j)),
                  pl.BlockSpec((None, 1, tn), lambda l, j: (l, 0, j))],
        out_specs=pl.BlockSpec((None, nc, tn), lambda l, j: (l, 0, j)),
        out_shape=jax.ShapeDtypeStruct((nl, nc, n), f32),
        compiler_params=_cp("parallel", "parallel"), name="modulation",
    )(cond, w_mod, b_mod.reshape(nl, 1, n))


def _norm_mod_kernel(x_ref, g_ref, mod_ref, o_ref):
    y = _rms(x_ref[...], g_ref[...])
    o_ref[...] = (y * (1.0 + mod_ref[1:2, :]) + mod_ref[0:1, :]).astype(o_ref.dtype)


def _norm_mod(x, gain, mods, rows):
    t, d = x.shape
    tm = rows.tm
    return pl.pallas_call(
        _norm_mod_kernel, grid=(rows.n,),
        in_specs=[pl.BlockSpec((tm, d), lambda i: (rows.src(i), 0)),
                  pl.BlockSpec((1, d), lambda i: (0, 0)),
                  pl.BlockSpec((None, 6, d), lambda i: (rows.cond(i), 0, 0))],
        out_specs=pl.BlockSpec((tm, d), lambda i: (rows.src(i), 0)),
        out_shape=jax.ShapeDtypeStruct((t, d), bf16),
        compiler_params=_cp("parallel"), name="norm_mod",
    )(x, gain.reshape(1, d), mods)


def _mm_kernel(a_ref, b_ref, o_ref):
    o_ref[...] = _dot(a_ref[...], b_ref[...]).astype(o_ref.dtype)


def _matmul(a, b, tm, tn, out_dtype, name):
    m, k = a.shape
    n = b.shape[1]
    return pl.pallas_call(
        _mm_kernel, grid=(m // tm, n // tn),
        in_specs=[pl.BlockSpec((tm, k), lambda i, j: (i, 0)),
                  pl.BlockSpec((k, tn), lambda i, j: (0, j))],
        out_specs=pl.BlockSpec((tm, tn), lambda i, j: (i, j)),
        out_shape=jax.ShapeDtypeStruct((m, n), out_dtype),
        compiler_params=_cp("parallel", "parallel"), name=name,
    )(a, b)


def _ssd_conv_kernel(x_ref, w_ref, b_ref, o_ref, *, LC, LL):
    s = LC + LL
    x = x_ref[...].astype(f32)
    row = lax.broadcasted_iota(jnp.int32, (s, 1), 0)
    pos = jnp.where(row < LC, row, row - LC)
    length = jnp.where(row < LC, LC, LL)
    acc = jnp.zeros_like(x) + b_ref[...]
    half = SSD_CONV // 2
    for k in range(SSD_CONV):
        sh = k - half
        xs = x if sh == 0 else pltpu.roll(x, (-sh) % s, 0)
        valid = (pos + sh >= 0) & (pos + sh < length)
        acc = acc + w_ref[k:k + 1, :] * jnp.where(valid, xs, 0.0)
    o_ref[...] = _silu(acc).astype(o_ref.dtype)


def _ssd_conv(p3, conv_w, conv_b, LC, LL):
    b, s, _ = p3.shape
    n = conv_w.shape[1]
    tc = 256
    base = _DST['ssd_x'] // tc
    return pl.pallas_call(
        functools.partial(_ssd_conv_kernel, LC=LC, LL=LL), grid=(b, n // tc),
        in_specs=[pl.BlockSpec((None, s, tc), lambda i, j: (i, 0, base + j)),
                  pl.BlockSpec((SSD_CONV, tc), lambda i, j: (0, j)),
                  pl.BlockSpec((1, tc), lambda i, j: (0, j))],
        out_specs=pl.BlockSpec((None, s, tc), lambda i, j: (i, 0, j)),
        out_shape=jax.ShapeDtypeStruct((b, s, n), bf16),
        compiler_params=_cp("parallel", "parallel"), name="ssd_conv",
    )(p3, conv_w, conv_b.reshape(1, n))


def _chunk_start(c, n_ctx, n_all, q, rev):
    if rev:
        c = jnp.where(c < n_ctx, n_ctx - 1 - c, n_all - 1 - (c - n_ctx))
    return pl.multiple_of(c * q, q)


def _pick4(hid, vals):
    return jnp.where(hid == 0, vals[0], jnp.where(hid == 1, vals[1], jnp.where(hid == 2, vals[2], vals[3])))


def _ssd_scan_kernel(x_ref, b_ref, c_ref, dt_ref, bias_ref, a_ref, o_ref, st_ref, *, LC, LL):
    q = SSD_CHUNK
    r = SSD_HEADS // SSD_GROUPS
    g = pl.program_id(1)
    n_ctx, n_all = LC // q, (LC + LL) // q
    ii = lax.broadcasted_iota(jnp.int32, (q, q), 0)
    jj = lax.broadcasted_iota(jnp.int32, (q, q), 1)
    lane = lax.broadcasted_iota(jnp.int32, (1, LANE), 1)
    sub = lax.broadcasted_iota(jnp.int32, (LANE, 1), 0)
    hid = lax.broadcasted_iota(jnp.int32, (1, r * SSD_HEAD_DIM), 1) // SSD_HEAD_DIM
    lane_lo = lane < SSD_HEAD_DIM

    for d in range(2):
        rev = d == 1
        tri = (ii <= jj) if rev else (ii >= jj)
        tri_f = tri.astype(f32)
        last = 0 if rev else q - 1
        st_ref[...] = jnp.zeros_like(st_ref)

        def chunk(c, carry, d=d, rev=rev, tri=tri, tri_f=tri_f, last=last):
            s = _chunk_start(c, n_ctx, n_all, q, rev)
            raw = dt_ref[pl.ds(s, q), :].astype(f32) + bias_ref[d:d + 1, :]
            dtv = jnp.maximum(raw, 0.0) + jnp.log1p(jnp.exp(-jnp.abs(raw)))
            a = dtv * a_ref[d:d + 1, :]
            acum = _dot_exact(tri_f, a)
            acum_t = acum.T
            x = x_ref[pl.ds(s, q), :].astype(f32)
            bm = b_ref[pl.ds(s, q), :]
            cm = c_ref[pl.ds(s, q), :]
            cb = _dot_nt(cm, bm)
            acol, arow, dcol, alast = [], [], [], []
            for hh in range(r):
                col = _DT_LANE + d * SSD_HEADS + g * r + hh
                acol.append(jnp.sum(jnp.where(lane == col, acum, 0.0), axis=1, keepdims=True))
                arow.append(jnp.sum(jnp.where(sub == col, acum_t, 0.0), axis=0, keepdims=True))
                dcol.append(jnp.sum(jnp.where(lane == col, dtv, 0.0), axis=1, keepdims=True))
                alast.append(acol[hh][last:last + 1, :])
            a_all = _pick4(hid, acol)
            al_all = _pick4(hid, alast)
            dtx = _pick4(hid, dcol) * x
            st = st_ref[...]
            y_off = _dot(cm, st.astype(bf16)) * jnp.exp(a_all)
            y_diag = []
            for pp in range(r // 2):
                decay = [jnp.exp(jnp.where(tri, acol[h] - arow[h], -1e30)) * cb for h in (2 * pp, 2 * pp + 1)]
                lhs = jnp.concatenate(decay, axis=1).astype(bf16)
                xp = dtx[:, pp * LANE:(pp + 1) * LANE]
                rhs = jnp.concatenate([jnp.where(lane_lo, xp, 0.0), jnp.where(lane_lo, 0.0, xp)], axis=0)
                y_diag.append(_dot(lhs, rhs.astype(bf16)))
            y = y_off + jnp.concatenate(y_diag, axis=1)
            if rev:
                o_ref[pl.ds(s, q), :] += y
            else:
                o_ref[pl.ds(s, q), :] = y
            w = (jnp.exp(al_all - a_all) * dtx).astype(bf16)
            st_ref[...] = st * jnp.exp(al_all) + _dot_tn(bm, w)
            return carry

        lax.fori_loop(0, n_all, chunk, 0)


def _ssd_scan(xbc3, p3, bias_v, a_v, LC, LL):
    b, s, _ = xbc3.shape
    d_in = SSD_HEADS * SSD_HEAD_DIM
    gw = d_in // SSD_GROUPS
    nb = d_in // SSD_STATE
    return pl.pallas_call(
        functools.partial(_ssd_scan_kernel, LC=LC, LL=LL), grid=(b, SSD_GROUPS),
        in_specs=[pl.BlockSpec((None, s, gw), lambda i, g: (i, 0, g)),
                  pl.BlockSpec((None, s, SSD_STATE), lambda i, g: (i, 0, nb + g)),
                  pl.BlockSpec((None, s, SSD_STATE), lambda i, g: (i, 0, nb + SSD_GROUPS + g)),
                  pl.BlockSpec((None, s, LANE), lambda i, g: (i, 0, _MISC // LANE)),
                  pl.BlockSpec((2, LANE), lambda i, g: (0, 0)),
                  pl.BlockSpec((2, LANE), lambda i, g: (0, 0))],
        out_specs=pl.BlockSpec((None, s, gw), lambda i, g: (i, 0, g)),
        out_shape=jax.ShapeDtypeStruct((b, s, d_in), f32),
        scratch_shapes=[pltpu.VMEM((SSD_STATE, gw), f32)],
        compiler_params=_cp("parallel", "parallel"), name="ssd_scan",
    )(xbc3, xbc3, xbc3, p3, bias_v, a_v)


def _ssd_finish_kernel(y_ref, x_ref, z_ref, dsk_ref, g_ref, o_ref):
    y = y_ref[...] + x_ref[...].astype(f32) * dsk_ref[...]
    o_ref[...] = _rms(y * _silu(z_ref[...].astype(f32)), g_ref[...]).astype(o_ref.dtype)


def _ssd_finish(y, xbc, p, d_skip_row, norm_g):
    t, n = y.shape
    tm = ROW_TILE
    zb = _DST['ssd_z'] // n
    return pl.pallas_call(
        _ssd_finish_kernel, grid=(t // tm,),
        in_specs=[pl.BlockSpec((tm, n), lambda i: (i, 0)),
                  pl.BlockSpec((tm, n), lambda i: (i, 0)),
                  pl.BlockSpec((tm, n), lambda i: (i, zb)),
                  pl.BlockSpec((1, n), lambda i: (0, 0)),
                  pl.BlockSpec((1, n), lambda i: (0, 0))],
        out_specs=pl.BlockSpec((tm, n), lambda i: (i, 0)),
        out_shape=jax.ShapeDtypeStruct((t, n), bf16),
        compiler_params=_cp("parallel"), name="ssd_finish",
    )(y, xbc, p, d_skip_row, norm_g.reshape(1, n))


def _rope(x, cos, sin):
    lane = lax.broadcasted_iota(jnp.int32, (1, LANE), 1)
    first = (lane % 64) < 32
    partner = jnp.where(first, pltpu.roll(x, LANE - 32, 1), pltpu.roll(x, 32, 1))
    return x * cos + partner * sin


def _swa_kernel(sink_ref, q_ref, k_ref, v_ref, cos_ref, sin_ref, o_ref, *, LC, LL, j0):
    w = WINDOW
    hd = SWA_HEAD_DIM
    grp = SWA_HEADS // SWA_KV_HEADS
    kv_w = SWA_KV_HEADS * hd
    jb = pl.program_id(1) + j0
    n_ctx = LC // w
    is_lat = jb >= n_ctx
    nb = jnp.maximum(jb - n_ctx, 0)
    qs = pl.multiple_of(jb * w, w)
    k0 = pl.multiple_of(jnp.clip((nb - 1) * w, 0, LL - 3 * w), w)
    ks = pl.multiple_of(LC + k0, w)
    scale = hd ** -0.5

    cos_q, sin_q = cos_ref[pl.ds(qs, w), :], sin_ref[pl.ds(qs, w), :]
    cos_k, sin_k = cos_ref[pl.ds(ks, 3 * w), :], sin_ref[pl.ds(ks, 3 * w), :]
    q = q_ref[...].astype(f32)
    q = jnp.concatenate([_rope(q[:, c * LANE:(c + 1) * LANE], cos_q, sin_q) for c in range(SWA_HEADS * hd // LANE)],
                        axis=1) * scale
    kb = k_ref[pl.ds(ks, 3 * w), :].astype(f32)
    kb = jnp.concatenate([_rope(kb[:, c * LANE:(c + 1) * LANE], cos_k, sin_k) for c in range(kv_w // LANE)],
                         axis=1).astype(bf16)
    vb = v_ref[pl.ds(ks, 3 * w), :]
    kc = k_ref[0:LC, :]
    vc = v_ref[0:LC, :]

    rows = grp * w
    q_abs = nb * w + lax.broadcasted_iota(jnp.int32, (rows, 1), 0) % w
    k_abs = k0 + lax.broadcasted_iota(jnp.int32, (1, 3 * w), 1)
    valid = is_lat & (jnp.abs(k_abs - q_abs) <= w)
    head_of_row = lax.broadcasted_iota(jnp.int32, (rows, 1), 0) // w

    outs = []
    for g in range(SWA_KV_HEADS):
        qg = jnp.concatenate([q[:, (g * grp + r) * hd:(g * grp + r + 1) * hd] for r in range(grp)], axis=0).astype(bf16)
        sl = slice(g * hd, (g + 1) * hd)
        s_lat = jnp.where(valid, _dot_nt(qg, kb[:, sl]), -1e30)
        s_ctx = _dot_nt(qg, kc[:, sl])
        sink = jnp.zeros((rows, 1), f32)
        for r in range(grp):
            sink = jnp.where(head_of_row == r, sink_ref[g * grp + r], sink)
        m = jnp.maximum(jnp.maximum(jnp.max(s_lat, axis=1, keepdims=True), jnp.max(s_ctx, axis=1, keepdims=True)), sink)
        e_lat = jnp.exp(s_lat - m)
        e_ctx = jnp.exp(s_ctx - m)
        den = jnp.sum(e_lat, axis=1, keepdims=True) + jnp.sum(e_ctx, axis=1, keepdims=True) + jnp.exp(sink - m)
        og = (_dot(e_lat.astype(bf16), vb[:, sl]) + _dot(e_ctx.astype(bf16), vc[:, sl])) / den
        outs.extend(og[r * w:(r + 1) * w, :] for r in range(grp))
    o_ref[...] = jnp.concatenate(outs, axis=1).astype(o_ref.dtype)


def _swa(p3, sink, cos_s, sin_s, LC, LL, lat_only):
    b, s, _ = p3.shape
    w = WINDOW
    j0 = LC // w if lat_only else 0
    nq = s // w - j0
    qw = SWA_HEADS * SWA_HEAD_DIM
    kw = SWA_KV_HEADS * SWA_HEAD_DIM
    return pl.pallas_call(
        functools.partial(_swa_kernel, LC=LC, LL=LL, j0=j0), grid=(b, nq),
        in_specs=[pl.BlockSpec(memory_space=pltpu.SMEM),
                  pl.BlockSpec((None, w, qw), lambda i, j: (i, j + j0, _DST['swa_q'] // qw)),
                  pl.BlockSpec((None, s, kw), lambda i, j: (i, 0, _DST['swa_k'] // kw)),
                  pl.BlockSpec((None, s, kw), lambda i, j: (i, 0, _DST['swa_v'] // kw)),
                  pl.BlockSpec((s, LANE), lambda i, j: (0, 0)),
                  pl.BlockSpec((s, LANE), lambda i, j: (0, 0))],
        out_specs=pl.BlockSpec((None, w, qw), lambda i, j: (i, j + j0, 0)),
        out_shape=jax.ShapeDtypeStruct((b, s, qw), bf16),
        compiler_params=_cp("parallel", "parallel"), name="swa",
    )(sink, p3, p3, p3, cos_s, sin_s)


def _hgrn_kernel(q_ref, ff_ref, fr_ref, v_ref, g_ref, lb_ref, ng_ref, o_ref, acc_ref, st_ref, *, LC, LL):
    q = HG_CHUNK
    n_ctx, n_all = LC // q, (LC + LL) // q
    ii = lax.broadcasted_iota(jnp.int32, (q, q), 0)
    jj = lax.broadcasted_iota(jnp.int32, (q, q), 1)
    r8 = lax.broadcasted_iota(jnp.int32, (q, 1), 0) % HG_DIAG
    levels = []
    sz = HG_DIAG
    while sz < q:
        levels.append(sz)
        sz *= 2

    for d in range(2):
        rev = d == 1
        tri_f = ((ii <= jj) if rev else (ii >= jj)).astype(f32)
        last = 0 if rev else q - 1
        f_ref = fr_ref if rev else ff_ref
        st_ref[...] = jnp.zeros_like(st_ref)
        masks = []
        for sz in levels:
            same = (ii // (2 * sz)) == (jj // (2 * sz))
            i_hi, j_hi = (ii % (2 * sz)) >= sz, (jj % (2 * sz)) >= sz
            masks.append(same & (~i_hi & j_hi if rev else i_hi & ~j_hi))

        def chunk(c, carry, d=d, rev=rev, tri_f=tri_f, last=last, f_ref=f_ref, masks=masks):
            s = _chunk_start(c, n_ctx, n_all, q, rev)
            lb = lb_ref[d:d + 1, :]
            qq = _silu(q_ref[pl.ds(s, q), :].astype(f32))
            f = lb + (1.0 - lb) * _sigmoid(f_ref[pl.ds(s, q), :].astype(f32))
            kk = 1.0 - f
            v = v_ref[pl.ds(s, q), :].astype(f32)
            cum = _dot_exact(tri_f, jnp.log(f))
            cl = cum[last:last + 1, :]
            st = st_ref[...]
            o = _dot_nt((qq * jnp.exp(cum)).astype(bf16), st.astype(bf16))
            att = jnp.zeros((q, q), f32)
            for sz, mask in zip(levels, masks):
                pieces = []
                for b0 in range(0, q, 2 * sz):
                    rr = b0 + sz if rev else b0 + sz - 1
                    pieces.append(jnp.broadcast_to(cum[rr:rr + 1, :], (2 * sz, cum.shape[1])))
                e = jnp.exp(-jnp.abs(cum - jnp.concatenate(pieces, axis=0)))
                att = att + jnp.where(mask, _dot_nt((qq * e).astype(bf16), (kk * e).astype(bf16)), 0.0)
            o = o + _dot(att.astype(bf16), v.astype(bf16))
            o = o + jnp.sum(qq * kk, axis=1, keepdims=True) * v
            for dd in range(1, HG_DIAG):
                sh = q - dd if rev else dd
                ks, cs, vs = pltpu.roll(kk, sh, 0), pltpu.roll(cum, sh, 0), pltpu.roll(v, sh, 0)
                ok = (r8 <= HG_DIAG - 1 - dd) if rev else (r8 >= dd)
                pr = jnp.where(ok, qq * ks * jnp.exp(cum - cs), 0.0)
                o = o + jnp.sum(pr, axis=1, keepdims=True) * vs
            st_ref[...] = st * jnp.exp(cl) + _dot_tn(v.astype(bf16), (kk * jnp.exp(cl - cum)).astype(bf16))
            if rev:
                y = _rms(acc_ref[pl.ds(s, q), :] + o, ng_ref[...])
                o_ref[pl.ds(s, q), :] = (y * _sigmoid(g_ref[pl.ds(s, q), :].astype(f32))).astype(o_ref.dtype)
            else:
                acc_ref[pl.ds(s, q), :] = o
            return carry

        lax.fori_loop(0, n_all, chunk, 0)


def _hgrn(p3, lb, norm_g, LC, LL):
    b, s, _ = p3.shape
    dk = HG_KEY_DIM
    n = HG_HEADS * dk

    def col(name, extra=0):
        base = _DST[name] // dk + extra
        return pl.BlockSpec((None, s, dk), lambda i, h: (i, 0, base + h))

    return pl.pallas_call(
        functools.partial(_hgrn_kernel, LC=LC, LL=LL), grid=(b, HG_HEADS),
        in_specs=[col('hg_q'), col('hg_f'), col('hg_f', HG_HEADS), col('hg_i'), col('hg_g'),
                  pl.BlockSpec((2, dk), lambda i, h: (0, h)),
                  pl.BlockSpec((1, dk), lambda i, h: (0, h))],
        out_specs=pl.BlockSpec((None, s, dk), lambda i, h: (i, 0, h)),
        out_shape=jax.ShapeDtypeStruct((b, s, n), bf16),
        scratch_shapes=[pltpu.VMEM((s, dk), f32), pltpu.VMEM((HG_VAL_DIM, dk), f32)],
        compiler_params=_cp("parallel", "parallel"), name="hgrn",
    )(p3, p3, p3, p3, p3, lb, norm_g.reshape(1, n))


def _mla_prep_kernel(cq_ref, ckv_ref, misc_ref, qg_ref, kvg_ref, wq_ref, wkv_ref, cos_ref, sin_ref,
                     q_out, k_out, v_out):
    h = MLA_HEADS
    scale = (MLA_NOPE + MLA_ROPE) ** -0.5
    cos, sin = cos_ref[...], sin_ref[...]
    qf = _dot(_rms(cq_ref[...].astype(f32), qg_ref[...]).astype(bf16), wq_ref[...]) * scale
    kv = _dot(_rms(ckv_ref[...].astype(f32), kvg_ref[...]).astype(bf16), wkv_ref[...])
    kr = _rope(misc_ref[...].astype(f32), cos, sin)[:, 0:MLA_ROPE].astype(bf16)
    ro = h * MLA_NOPE
    for pp in range(h * MLA_ROPE // LANE):
        qr = _rope(qf[:, ro + pp * LANE:ro + (pp + 1) * LANE], cos, sin).astype(bf16)
        for e in range(LANE // MLA_ROPE):
            q_out[pp * (LANE // MLA_ROPE) + e, :, MLA_NOPE:MLA_NOPE + MLA_ROPE] = qr[:, e * MLA_ROPE:(e + 1) * MLA_ROPE]
    for hh in range(h):
        q_out[hh, :, 0:MLA_NOPE] = qf[:, hh * MLA_NOPE:(hh + 1) * MLA_NOPE].astype(bf16)
        k_out[hh, :, 0:MLA_NOPE] = kv[:, hh * MLA_NOPE:(hh + 1) * MLA_NOPE].astype(bf16)
        k_out[hh, :, MLA_NOPE:MLA_NOPE + MLA_ROPE] = kr
        v_out[hh] = kv[:, ro + hh * MLA_V:ro + (hh + 1) * MLA_V].astype(bf16)


def _mla_prep(p, q_norm_g, kv_norm_g, w_uq_p, w_ukv_p, cos_s, sin_s, S):
    t = p.shape[0]
    tm = ROW_TILE
    ns = S // tm
    h, qd = MLA_HEADS, MLA_NOPE + MLA_ROPE
    return pl.pallas_call(
        _mla_prep_kernel, grid=(t // tm,),
        in_specs=[pl.BlockSpec((tm, MLA_Q_RANK), lambda i: (i, _DST['mla_cq'] // MLA_Q_RANK)),
                  pl.BlockSpec((tm, MLA_KV_RANK), lambda i: (i, _DST['mla_ckv'] // MLA_KV_RANK)),
                  pl.BlockSpec((tm, LANE), lambda i: (i, _MISC // LANE)),
                  pl.BlockSpec((1, MLA_Q_RANK), lambda i: (0, 0)),
                  pl.BlockSpec((1, MLA_KV_RANK), lambda i: (0, 0)),
                  pl.BlockSpec(w_uq_p.shape, lambda i: (0, 0)),
                  pl.BlockSpec(w_ukv_p.shape, lambda i: (0, 0)),
                  pl.BlockSpec((tm, LANE), lambda i: (i % ns, 0)),
                  pl.BlockSpec((tm, LANE), lambda i: (i % ns, 0))],
        out_specs=[pl.BlockSpec((h, tm, qd), lambda i: (0, i, 0)),
                   pl.BlockSpec((h, tm, qd), lambda i: (0, i, 0)),
                   pl.BlockSpec((h, tm, MLA_V), lambda i: (0, i, 0))],
        out_shape=[jax.ShapeDtypeStruct((h, t, qd), bf16), jax.ShapeDtypeStruct((h, t, qd), bf16),
                   jax.ShapeDtypeStruct((h, t, MLA_V), bf16)],
        compiler_params=_cp("parallel"), name="mla_prep",
    )(p, p, p, q_norm_g.reshape(1, -1), kv_norm_g.reshape(1, -1), w_uq_p, w_ukv_p, cos_s, sin_s)


def _mla_attn_kernel(q_ref, k_ref, v_ref, o_ref, *, LC, j0, tq):
    jb = pl.program_id(2) + j0
    s = _dot_nt(q_ref[...], k_ref[...])
    key = lax.broadcasted_iota(jnp.int32, (1, s.shape[1]), 1)
    s = jnp.where((jb < LC // tq) & (key >= LC), -1e30, s)
    e = jnp.exp(s - jnp.max(s, axis=1, keepdims=True))
    o = _dot(e.astype(bf16), v_ref[...]) / jnp.sum(e, axis=1, keepdims=True)
    o_ref[...] = o.astype(o_ref.dtype)


def _mla_attn(qh, kh, vh, B, LC, LL, lat_only):
    h, t, qd = qh.shape
    s = LC + LL
    tq = ROW_TILE
    j0 = LC // tq if lat_only else 0
    nq = s // tq - j0
    ns = s // tq
    return pl.pallas_call(
        functools.partial(_mla_attn_kernel, LC=LC, j0=j0, tq=tq), grid=(B, h, nq),
        in_specs=[pl.BlockSpec((None, tq, qd), lambda b, hh, j: (hh, b * ns + j + j0, 0)),
                  pl.BlockSpec((None, s, qd), lambda b, hh, j: (hh, b, 0)),
                  pl.BlockSpec((None, s, MLA_V), lambda b, hh, j: (hh, b, 0))],
        out_specs=pl.BlockSpec((tq, MLA_V), lambda b, hh, j: (b * ns + j + j0, hh)),
        out_shape=jax.ShapeDtypeStruct((t, h * MLA_V), bf16),
        compiler_params=_cp("parallel", "parallel", "parallel"), name="mla_attn",
    )(qh, kh, vh)


def _merge_kernel(y0, y1, y2, y3, g0, g1, g2, g3, w_ref, o_ref):
    acc = None
    for n, (y_ref, g_ref) in enumerate(((y0, g0), (y1, g1), (y2, g2), (y3, g3))):
        t = _sigmoid(g_ref[...].astype(f32)) * _dot(y_ref[...], w_ref[n])
        acc = t if acc is None else acc + t
    o_ref[...] = acc.astype(o_ref.dtype)


def _merge(ys, p, w_branch_b, rows):
    t = p.shape[0]
    d = D_MODEL
    tm, tn = rows.tm, 512
    gb = _DST['gates'] // tn
    y_specs = [pl.BlockSpec((tm, BRANCH_W), lambda i, j: (rows.src(i), 0)) for _ in range(N_BRANCH)]
    g_specs = [pl.BlockSpec((tm, tn), functools.partial(lambda i, j, n: (rows.src(i), gb + n * (d // tn) + j), n=n))
               for n in range(N_BRANCH)]
    return pl.pallas_call(
        _merge_kernel, grid=(rows.n, d // tn),
        in_specs=y_specs + g_specs + [pl.BlockSpec((N_BRANCH, BRANCH_W, tn), lambda i, j: (0, 0, j))],
        out_specs=pl.BlockSpec((tm, tn), lambda i, j: (i, j)),
        out_shape=jax.ShapeDtypeStruct((rows.n * tm, d), bf16),
        compiler_params=_cp("parallel", "parallel"), name="merge",
    )(*ys, p, p, p, p, w_branch_b)


def _out_proj_kernel(m_ref, w_ref, x_ref, mod_ref, gpost_ref, gffn_ref, rw_ref, x_out, h_out, l_out):
    r = _rms(_dot(m_ref[...], w_ref[...]), gpost_ref[...])
    x = x_ref[...] + mod_ref[2:3, :] * r
    x_out[...] = x
    h2 = _rms(x, gffn_ref[...]) * (1.0 + mod_ref[4:5, :]) + mod_ref[3:4, :]
    h_out[...] = h2.astype(h_out.dtype)
    l_out[...] = _dot_exact(h2, rw_ref[...])


def _out_proj(m, w_out_b, x, mods, g_post, g_ffn, router_w, rows):
    d = D_MODEL
    tm = rows.tm
    n = rows.n * tm
    ne = router_w.shape[1]
    return pl.pallas_call(
        _out_proj_kernel, grid=(rows.n,),
        in_specs=[pl.BlockSpec((tm, d), lambda i: (i, 0)),
                  pl.BlockSpec((d, d), lambda i: (0, 0)),
                  pl.BlockSpec((tm, d), lambda i: (rows.src(i), 0)),
                  pl.BlockSpec((None, 6, d), lambda i: (rows.cond(i), 0, 0)),
                  pl.BlockSpec((1, d), lambda i: (0, 0)),
                  pl.BlockSpec((1, d), lambda i: (0, 0)),
                  pl.BlockSpec((d, ne), lambda i: (0, 0))],
        out_specs=[pl.BlockSpec((tm, d), lambda i: (i, 0)),
                   pl.BlockSpec((tm, d), lambda i: (i, 0)),
                   pl.BlockSpec((tm, ne), lambda i: (i, 0))],
        out_shape=[jax.ShapeDtypeStruct((n, d), f32), jax.ShapeDtypeStruct((n, d), bf16),
                   jax.ShapeDtypeStruct((n, ne), f32)],
        compiler_params=_cp("parallel"), name="out_proj",
    )(m, w_out_b, x, mods, g_post.reshape(1, d), g_ffn.reshape(1, d), router_w)


def _expert_kernel(be_ref, nu_ref, x_ref, w_ref, wg_ref, wu_ref, wd_ref, o_ref):
    i = pl.program_id(0)

    @pl.when(i < nu_ref[0])
    def _():
        x = x_ref[...]
        hmid = _silu(_dot(x, wg_ref[...])) * _dot(x, wu_ref[...])
        o_ref[...] = (_dot(hmid.astype(bf16), wd_ref[...]) * w_ref[...]).astype(o_ref.dtype)

    @pl.when(i >= nu_ref[0])
    def _():
        o_ref[...] = jnp.zeros_like(o_ref)


def _experts(x_sorted, row_w, blk_e, n_used, wg, wu, wd):
    n_rows, d = x_sorted.shape
    ff = wg.shape[2]
    bm = MOE_ROWS
    grid_spec = pltpu.PrefetchScalarGridSpec(
        num_scalar_prefetch=2, grid=(n_rows // bm,),
        in_specs=[pl.BlockSpec((bm, d), lambda i, be, nu: (i, 0)),
                  pl.BlockSpec((bm, 1), lambda i, be, nu: (i, 0)),
                  pl.BlockSpec((None, d, ff), lambda i, be, nu: (be[i], 0, 0)),
                  pl.BlockSpec((None, d, ff), lambda i, be, nu: (be[i], 0, 0)),
                  pl.BlockSpec((None, ff, d), lambda i, be, nu: (be[i], 0, 0))],
        out_specs=pl.BlockSpec((bm, d), lambda i, be, nu: (i, 0)))
    return pl.pallas_call(
        _expert_kernel, grid_spec=grid_spec,
        out_shape=jax.ShapeDtypeStruct((n_rows, d), bf16),
        compiler_params=_cp("arbitrary"), name="experts",
    )(blk_e, n_used, x_sorted, row_w, wg, wu, wd)


def _ffn_out_kernel(h_ref, r_ref, x_ref, mod_ref, g_ref, sg_ref, su_ref, sd_ref, o_ref):
    h = h_ref[...]
    mid = _silu(_dot(h, sg_ref[...])) * _dot(h, su_ref[...])
    f = _dot(mid.astype(bf16), sd_ref[...]) + r_ref[...]
    o_ref[...] = x_ref[...] + mod_ref[5:6, :] * _rms(f, g_ref[...])


def _ffn_out(h2, routed, x, mods, g_post, sg, su, sd, rows):
    n, d = h2.shape
    tm = rows.tm
    ff = sg.shape[1]
    return pl.pallas_call(
        _ffn_out_kernel, grid=(rows.n,),
        in_specs=[pl.BlockSpec((tm, d), lambda i: (i, 0)),
                  pl.BlockSpec((tm, d), lambda i: (i, 0)),
                  pl.BlockSpec((tm, d), lambda i: (i, 0)),
                  pl.BlockSpec((None, 6, d), lambda i: (rows.cond(i), 0, 0)),
                  pl.BlockSpec((1, d), lambda i: (0, 0)),
                  pl.BlockSpec((d, ff), lambda i: (0, 0)),
                  pl.BlockSpec((d, ff), lambda i: (0, 0)),
                  pl.BlockSpec((ff, d), lambda i: (0, 0))],
        out_specs=pl.BlockSpec((tm, d), lambda i: (i, 0)),
        out_shape=jax.ShapeDtypeStruct((n, d), f32),
        compiler_params=_cp("parallel"), name="ffn_out",
    )(h2, routed, x, mods, g_post.reshape(1, d), sg, su, sd)


def _route(logits, router_bias):
    t = logits.shape[0]
    scores = jax.nn.sigmoid(logits)
    sel = scores + router_bias.astype(f32)
    per_group = N_EXPERTS // N_EXPERT_GROUPS
    group_score = lax.top_k(sel.reshape(t, N_EXPERT_GROUPS, per_group), 2)[0].sum(-1)
    _, top_groups = lax.top_k(group_score, TOPK_EXPERT_GROUPS)
    group_mask = jax.nn.one_hot(top_groups, N_EXPERT_GROUPS, dtype=f32).sum(1) > 0
    sel = jnp.where(jnp.repeat(group_mask, per_group, axis=1), sel, -jnp.inf)
    _, top_e = lax.top_k(sel, TOP_K)
    top_w = jnp.take_along_axis(scores, top_e, axis=1)
    top_w = top_w / jnp.sum(top_w, axis=-1, keepdims=True) * ROUTE_SCALE
    return top_e, top_w


def _moe(h2, logits, router_bias, wg, wu, wd):
    t, d = h2.shape
    top_e, top_w = _route(logits, router_bias)
    n_assign = t * TOP_K
    flat_e = top_e.reshape(n_assign)
    onehot = (flat_e[:, None] == jnp.arange(N_EXPERTS)[None, :]).astype(jnp.int32)
    csum = jnp.cumsum(onehot, axis=0)
    counts = csum[-1]
    rank = jnp.take_along_axis(csum, flat_e[:, None], axis=1)[:, 0] - 1
    padded = (counts + MOE_ROWS - 1) // MOE_ROWS * MOE_ROWS
    pad_end = jnp.cumsum(padded)
    dest = (pad_end - padded)[flat_e] + rank
    n_blocks = -(-n_assign // MOE_ROWS) + N_EXPERTS
    n_rows = n_blocks * MOE_ROWS
    row_tok = jnp.zeros((n_rows,), jnp.int32).at[dest].set(jnp.arange(n_assign, dtype=jnp.int32) // TOP_K)
    row_w = jnp.zeros((n_rows,), f32).at[dest].set(top_w.reshape(n_assign))
    blk_e = jnp.minimum(jnp.searchsorted(pad_end, jnp.arange(n_blocks) * MOE_ROWS, side='right'),
                        N_EXPERTS - 1).astype(jnp.int32)
    n_used = (pad_end[-1] // MOE_ROWS).astype(jnp.int32).reshape(1)
    x_sorted = jnp.take(h2, row_tok, axis=0)
    y_sorted = _experts(x_sorted, row_w.reshape(n_rows, 1), blk_e, n_used, wg, wu, wd)
    return jnp.take(y_sorted, dest, axis=0).reshape(t, TOP_K, d).astype(f32).sum(axis=1)


def _permute_w_in(w_in):
    cols = [w_in[:, _SRC[name][0]:_SRC[name][0] + _SRC[name][1]] for name in _P_ORDER]
    cols.append(jnp.zeros((w_in.shape[0], _P_WIDTH - _P_USED), w_in.dtype))
    return jnp.concatenate(cols, axis=1).astype(bf16)


def _permute_mla(w_uq, w_ukv):
    h = MLA_HEADS
    q = w_uq.reshape(-1, h, MLA_NOPE + MLA_ROPE)
    w_uq_p = jnp.concatenate([q[:, :, :MLA_NOPE].reshape(-1, h * MLA_NOPE), q[:, :, MLA_NOPE:].reshape(-1, h * MLA_ROPE)], 1)
    kv = w_ukv.reshape(-1, h, MLA_NOPE + MLA_V)
    w_ukv_p = jnp.concatenate([kv[:, :, :MLA_NOPE].reshape(-1, h * MLA_NOPE), kv[:, :, MLA_NOPE:].reshape(-1, h * MLA_V)], 1)
    return w_uq_p.astype(bf16), w_ukv_p.astype(bf16)


def _rope_tables(LC, LL):
    pos = jnp.arange(LL)
    n_freq = SWA_HEAD_DIM // 4
    inv_freq = ROPE_BASE ** (-jnp.arange(n_freq, dtype=f32) / n_freq)
    ang = jnp.concatenate([(pos // GRID_W).astype(f32)[:, None] * inv_freq,
                           (pos % GRID_W).astype(f32)[:, None] * inv_freq], axis=-1)
    cos, sin = jnp.cos(ang), jnp.sin(ang)
    cos = jnp.concatenate([jnp.ones((LC, cos.shape[1]), f32), cos], axis=0)
    sin = jnp.concatenate([jnp.zeros((LC, sin.shape[1]), f32), sin], axis=0)
    return jnp.tile(jnp.concatenate([cos, cos], 1), (1, 2)), jnp.tile(jnp.concatenate([-sin, sin], 1), (1, 2))


def _dir_lanes(v):
    out = jnp.zeros((2, LANE), f32)
    for d in range(2):
        out = out.at[d, _DT_LANE + d * SSD_HEADS:_DT_LANE + (d + 1) * SSD_HEADS].set(v[d].astype(f32))
    return out


def kernel(x, c, ctx, c_ctx, w_mod, b_mod, norm_pre_mix, norm_post_mix, norm_pre_ffn, norm_post_ffn, w_in, ssd_conv_w, ssd_conv_b, ssd_dt_bias, ssd_a_log, ssd_d, ssd_norm, swa_sink, hg_lb_logits, hg_norm, mla_q_norm, mla_kv_norm, mla_w_uq, mla_w_ukv, w_branch, w_out, router_w, router_bias, expert_w_gate, expert_w_up, expert_w_down, shared_w_gate, shared_w_up, shared_w_down):
    B, LL, d = x.shape
    LC = ctx.shape[1]
    S = LC + LL
    depth = w_mod.shape[0]
    assert LC % ROW_TILE == 0 and LL % ROW_TILE == 0 and LL >= 3 * WINDOW and d == D_MODEL

    n_cond = -(-(B + 1) // 8) * 8
    cond = jnp.zeros((n_cond, d), f32).at[:B].set(c).at[B].set(c_ctx)
    mods_all = _modulation(cond, w_mod, b_mod).reshape(depth, n_cond, 6, d)
    cos_s, sin_s = _rope_tables(LC, LL)
    sm = jax.nn.softmax(hg_lb_logits.astype(f32), axis=0)
    lower_bounds = jnp.cumsum(sm, axis=0) - sm[0]

    xs = jnp.concatenate([ctx, x], axis=1).reshape(B * S, d)
    for l in range(depth):
        last = l == depth - 1
        mods = mods_all[l]
        all_rows = _Rows(B, LC, LL, False)
        out_rows = _Rows(B, LC, LL, last)
        h = _norm_mod(xs, norm_pre_mix[l], mods, all_rows)
        p = _matmul(h, _permute_w_in(w_in[l]), 1024, _P_TN, bf16, "in_proj")
        p3 = p.reshape(B, S, _P_WIDTH)

        xbc3 = _ssd_conv(p3, ssd_conv_w[l], ssd_conv_b[l], LC, LL)
        y_ssd = _ssd_scan(xbc3, p3, _dir_lanes(ssd_dt_bias[l]), _dir_lanes(-jnp.exp(ssd_a_log[l].astype(f32))), LC, LL)
        d_in = SSD_HEADS * SSD_HEAD_DIM
        y_ssd = _ssd_finish(y_ssd.reshape(B * S, d_in), xbc3.reshape(B * S, -1), p,
                            jnp.repeat(ssd_d[l].astype(f32), SSD_HEAD_DIM).reshape(1, d_in), ssd_norm[l])
        y_swa = _swa(p3, swa_sink[l].astype(f32), cos_s, sin_s, LC, LL, last).reshape(B * S, -1)
        y_hg = _hgrn(p3, lower_bounds[l], hg_norm[l], LC, LL).reshape(B * S, -1)
        w_uq_p, w_ukv_p = _permute_mla(mla_w_uq[l], mla_w_ukv[l])
        qh, kh, vh = _mla_prep(p, mla_q_norm[l], mla_kv_norm[l], w_uq_p, w_ukv_p, cos_s, sin_s, S)
        y_mla = _mla_attn(qh, kh, vh, B, LC, LL, last)

        m = _merge((y_ssd, y_swa, y_hg, y_mla), p, w_branch[l].astype(bf16), out_rows)
        x_mid, h2, logits = _out_proj(m, w_out[l].astype(bf16), xs, mods, norm_post_mix[l], norm_pre_ffn[l],
                                      router_w[l], out_rows)
        routed = _moe(h2, logits, router_bias[l], expert_w_gate[l].astype(bf16), expert_w_up[l].astype(bf16),
                      expert_w_down[l].astype(bf16))
        xs = _ffn_out(h2, routed, x_mid, mods, norm_post_ffn[l], shared_w_gate[l].astype(bf16),
                      shared_w_up[l].astype(bf16), shared_w_down[l].astype(bf16), out_rows)
    return xs.reshape(B, LL, d)
```

```python
import functools
import math

import jax
import jax.numpy as jnp
import numpy as np
from jax import lax
from jax.experimental import pallas as pl
from jax.experimental.pallas import tpu as pltpu

f32 = jnp.float32
bf16 = jnp.bfloat16

D_MODEL = 2048
GRID_W = 64
N_BRANCH = 4
BRANCH_W = 1024
EPS = 1e-6
ROPE_BASE = 10000.0

SSD_HEADS = 16
SSD_HEAD_DIM = 64
SSD_GROUPS = 4
SSD_STATE = 128
SSD_CONV = 5
SSD_CHUNK = 128

SWA_HEADS = 16
SWA_KV_HEADS = 4
SWA_HEAD_DIM = 64
WINDOW = 128

HG_HEADS = 8
HG_KEY_DIM = 128
HG_VAL_DIM = 128
HG_CHUNK = 64
HG_DIAG = 8

MLA_HEADS = 8
MLA_Q_RANK = 512
MLA_KV_RANK = 256
MLA_NOPE = 128
MLA_ROPE = 64
MLA_V = 128

N_EXPERTS = 64
TOP_K = 6
N_EXPERT_GROUPS = 8
TOPK_EXPERT_GROUPS = 4
EXPERT_FF = 512
SHARED_FF = 1024
ROUTE_SCALE = 2.5
MOE_ROWS = 256

LANE = 128
ROW_TILE = 256
VMEM_LIMIT = 48 * 1024 * 1024

_IN_LAYOUT = (
    ('ssd_z', 1024), ('ssd_x', 1024), ('ssd_B', 512), ('ssd_C', 512), ('ssd_dt', 32),
    ('swa_q', 1024), ('swa_k', 256), ('swa_v', 256),
    ('hg_q', 1024), ('hg_f', 2048), ('hg_i', 1024), ('hg_g', 1024),
    ('mla_cq', 512), ('mla_ckv', 256), ('mla_kr', 64), ('gates', 8192),
)
_P_ORDER = ('gates', 'hg_f', 'ssd_x', 'ssd_B', 'ssd_C', 'ssd_z', 'swa_q', 'hg_q', 'hg_i', 'hg_g',
            'mla_cq', 'swa_k', 'swa_v', 'mla_ckv', 'mla_kr', 'ssd_dt')
_P_TN = 512


def _layout():
    src, off = {}, 0
    for name, size in _IN_LAYOUT:
        src[name] = (off, size)
        off += size
    dst, off = {}, 0
    for name in _P_ORDER:
        dst[name] = off
        off += src[name][1]
    width = -(-off // _P_TN) * _P_TN
    return src, dst, off, width


_SRC, _DST, _P_USED, _P_WIDTH = _layout()
_MISC = _DST['mla_kr']
_DT_LANE = _DST['ssd_dt'] - _MISC


def _cp(*sem):
    return pltpu.CompilerParams(dimension_semantics=sem, vmem_limit_bytes=VMEM_LIMIT)


def _sigmoid(x):
    return 1.0 / (1.0 + jnp.exp(-x))


def _silu(x):
    return x * _sigmoid(x)


def _rms(x, gain):
    return x * lax.rsqrt(jnp.mean(x * x, axis=-1, keepdims=True) + EPS) * gain


def _dot(a, b):
    return jnp.dot(a, b, preferred_element_type=f32)


def _dot_nt(a, b):
    return lax.dot_general(a, b, (((1,), (1,)), ((), ())), preferred_element_type=f32)


def _dot_tn(a, b):
    return lax.dot_general(a, b, (((0,), (0,)), ((), ())), preferred_element_type=f32)


def _dot_exact(a, b):
    return jnp.dot(a, b, preferred_element_type=f32, precision=lax.Precision.HIGHEST)


class _Rows:
    def __init__(self, B, LC, LL, lat_only, tm=ROW_TILE):
        self.B, self.LC, self.LL, self.tm = B, LC, LL, tm
        self.nC, self.nL = LC // tm, LL // tm
        self.nS = self.nC + self.nL
        self.lat_only = lat_only
        self.n = B * (self.nL if lat_only else self.nS)

    def src(self, i):
        if self.lat_only:
            return (i // self.nL) * self.nS + self.nC + i % self.nL
        return i

    def cond(self, i):
        if self.lat_only:
            return i // self.nL
        return jnp.where(i % self.nS < self.nC, self.B, i // self.nS)

    def pos(self, i):
        if self.lat_only:
            return self.nC + i % self.nL
        return i % self.nS


def _mod_kernel(c_ref, w_ref, b_ref, o_ref):
    s = _silu(c_ref[...]).astype(bf16)
    o_ref[...] = _dot(s, w_ref[...].astype(bf16)) + b_ref[...]


def _modulation(cond, w_mod, b_mod):
    nl, d, n = w_mod.shape
    nc = cond.shape[0]
    tn = 1024
    return pl.pallas_call(
        _mod_kernel, grid=(nl, n // tn),
        in_specs=[pl.BlockSpec((nc, d), lambda l, j: (0, 0)),
                  pl.BlockSpec((None, d, tn), lambda l, j: (l, 0, j)),
                  pl.BlockSpec((None, 1, tn), lambda l, j: (l, 0, j))],
        out_specs=pl.BlockSpec((None, nc, tn), lambda l, j: (l, 0, j)),
        out_shape=jax.ShapeDtypeStruct((nl, nc, n), f32),
        compiler_params=_cp("parallel", "parallel"), name="modulation",
    )(cond, w_mod, b_mod.reshape(nl, 1, n))


def _norm_mod_kernel(x_ref, g_ref, mod_ref, o_ref):
    y = _rms(x_ref[...], g_ref[...])
    o_ref[...] = (y * (1.0 + mod_ref[1:2, :]) + mod_ref[0:1, :]).astype(o_ref.dtype)


def _norm_mod(x, gain, mods, rows):
    t, d = x.shape
    tm = rows.tm
    return pl.pallas_call(
        _norm_mod_kernel, grid=(rows.n,),
        in_specs=[pl.BlockSpec((tm, d), lambda i: (rows.src(i), 0)),
                  pl.BlockSpec((1, d), lambda i: (0, 0)),
                  pl.BlockSpec((None, 6, d), lambda i: (rows.cond(i), 0, 0))],
        out_specs=pl.BlockSpec((tm, d), lambda i: (rows.src(i), 0)),
        out_shape=jax.ShapeDtypeStruct((t, d), bf16),
        compiler_params=_cp("parallel"), name="norm_mod",
    )(x, gain.reshape(1, d), mods)


def _mm_kernel(a_ref, b_ref, o_ref):
    o_ref[...] = _dot(a_ref[...], b_ref[...]).astype(o_ref.dtype)


def _matmul(a, b, tm, tn, out_dtype, name):
    m, k = a.shape
    n = b.shape[1]
    return pl.pallas_call(
        _mm_kernel, grid=(m // tm, n // tn),
        in_specs=[pl.BlockSpec((tm, k), lambda i, j: (i, 0)),
                  pl.BlockSpec((k, tn), lambda i, j: (0, j))],
        out_specs=pl.BlockSpec((tm, tn), lambda i, j: (i, j)),
        out_shape=jax.ShapeDtypeStruct((m, n), out_dtype),
        compiler_params=_cp("parallel", "parallel"), name=name,
    )(a, b)


def _ssd_conv_kernel(x_ref, w_ref, b_ref, o_ref, *, LC, LL):
    s = LC + LL
    x = x_ref[...].astype(f32)
    row = lax.broadcasted_iota(jnp.int32, (s, 1), 0)
    pos = jnp.where(row < LC, row, row - LC)
    length = jnp.where(row < LC, LC, LL)
    acc = jnp.zeros_like(x) + b_ref[...]
    half = SSD_CONV // 2
    for k in range(SSD_CONV):
        sh = k - half
        xs = x if sh == 0 else pltpu.roll(x, (-sh) % s, 0)
        valid = (pos + sh >= 0) & (pos + sh < length)
        acc = acc + w_ref[k:k + 1, :] * jnp.where(valid, xs, 0.0)
    o_ref[...] = _silu(acc).astype(o_ref.dtype)


def _ssd_conv(p3, conv_w, conv_b, LC, LL):
    b, s, _ = p3.shape
    n = conv_w.shape[1]
    tc = 256
    base = _DST['ssd_x'] // tc
    return pl.pallas_call(
        functools.partial(_ssd_conv_kernel, LC=LC, LL=LL), grid=(b, n // tc),
        in_specs=[pl.BlockSpec((None, s, tc), lambda i, j: (i, 0, base + j)),
                  pl.BlockSpec((SSD_CONV, tc), lambda i, j: (0, j)),
                  pl.BlockSpec((1, tc), lambda i, j: (0, j))],
        out_specs=pl.BlockSpec((None, s, tc), lambda i, j: (i, 0, j)),
        out_shape=jax.ShapeDtypeStruct((b, s, n), bf16),
        compiler_params=_cp("parallel", "parallel"), name="ssd_conv",
    )(p3, conv_w, conv_b.reshape(1, n))


def _chunk_start(c, n_ctx, n_all, q, rev):
    if rev:
        c = jnp.where(c < n_ctx, n_ctx - 1 - c, n_all - 1 - (c - n_ctx))
    return pl.multiple_of(c * q, q)


def _pick4(hid, vals):
    return jnp.where(hid == 0, vals[0], jnp.where(hid == 1, vals[1], jnp.where(hid == 2, vals[2], vals[3])))


def _ssd_scan_kernel(x_ref, b_ref, c_ref, dt_ref, bias_ref, a_ref, o_ref, st_ref, *, LC, LL):
    q = SSD_CHUNK
    r = SSD_HEADS // SSD_GROUPS
    g = pl.program_id(1)
    n_ctx, n_all = LC // q, (LC + LL) // q
    ii = lax.broadcasted_iota(jnp.int32, (q, q), 0)
    jj = lax.broadcasted_iota(jnp.int32, (q, q), 1)
    lane = lax.broadcasted_iota(jnp.int32, (1, LANE), 1)
    sub = lax.broadcasted_iota(jnp.int32, (LANE, 1), 0)
    hid = lax.broadcasted_iota(jnp.int32, (1, r * SSD_HEAD_DIM), 1) // SSD_HEAD_DIM
    lane_lo = lane < SSD_HEAD_DIM

    for d in range(2):
        rev = d == 1
        tri = (ii <= jj) if rev else (ii >= jj)
        tri_f = tri.astype(f32)
        last = 0 if rev else q - 1
        st_ref[...] = jnp.zeros_like(st_ref)

        def chunk(c, carry, d=d, rev=rev, tri=tri, tri_f=tri_f, last=last):
            s = _chunk_start(c, n_ctx, n_all, q, rev)
            raw = dt_ref[pl.ds(s, q), :].astype(f32) + bias_ref[d:d + 1, :]
            dtv = jnp.maximum(raw, 0.0) + jnp.log1p(jnp.exp(-jnp.abs(raw)))
            a = dtv * a_ref[d:d + 1, :]
            acum = _dot_exact(tri_f, a)
            acum_t = acum.T
            x = x_ref[pl.ds(s, q), :].astype(f32)
            bm = b_ref[pl.ds(s, q), :]
            cm = c_ref[pl.ds(s, q), :]
            cb = _dot_nt(cm, bm)
            acol, arow, dcol, alast = [], [], [], []
            for hh in range(r):
                col = _DT_LANE + d * SSD_HEADS + g * r + hh
                acol.append(jnp.sum(jnp.where(lane == col, acum, 0.0), axis=1, keepdims=True))
                arow.append(jnp.sum(jnp.where(sub == col, acum_t, 0.0), axis=0, keepdims=True))
                dcol.append(jnp.sum(jnp.where(lane == col, dtv, 0.0), axis=1, keepdims=True))
                alast.append(acol[hh][last:last + 1, :])
            a_all = _pick4(hid, acol)
            al_all = _pick4(hid, alast)
            dtx = _pick4(hid, dcol) * x
            st = st_ref[...]
            y_off = _dot(cm, st.astype(bf16)) * jnp.exp(a_all)
            y_diag = []
            for pp in range(r // 2):
                decay = [jnp.exp(jnp.where(tri, acol[h] - arow[h], -1e30)) * cb for h in (2 * pp, 2 * pp + 1)]
                lhs = jnp.concatenate(decay, axis=1).astype(bf16)
                xp = dtx[:, pp * LANE:(pp + 1) * LANE]
                rhs = jnp.concatenate([jnp.where(lane_lo, xp, 0.0), jnp.where(lane_lo, 0.0, xp)], axis=0)
                y_diag.append(_dot(lhs, rhs.astype(bf16)))
            y = y_off + jnp.concatenate(y_diag, axis=1)
            if rev:
                o_ref[pl.ds(s, q), :] += y
            else:
                o_ref[pl.ds(s, q), :] = y
            w = (jnp.exp(al_all - a_all) * dtx).astype(bf16)
            st_ref[...] = st * jnp.exp(al_all) + _dot_tn(bm, w)
            return carry

        lax.fori_loop(0, n_all, chunk, 0)


def _ssd_scan(xbc3, p3, bias_v, a_v, LC, LL):
    b, s, _ = xbc3.shape
    d_in = SSD_HEADS * SSD_HEAD_DIM
    gw = d_in // SSD_GROUPS
    nb = d_in // SSD_STATE
    return pl.pallas_call(
        functools.partial(_ssd_scan_kernel, LC=LC, LL=LL), grid=(b, SSD_GROUPS),
        in_specs=[pl.BlockSpec((None, s, gw), lambda i, g: (i, 0, g)),
                  pl.BlockSpec((None, s, SSD_STATE), lambda i, g: (i, 0, nb + g)),
                  pl.BlockSpec((None, s, SSD_STATE), lambda i, g: (i, 0, nb + SSD_GROUPS + g)),
                  pl.BlockSpec((None, s, LANE), lambda i, g: (i, 0, _MISC // LANE)),
                  pl.BlockSpec((2, LANE), lambda i, g: (0, 0)),
                  pl.BlockSpec((2, LANE), lambda i, g: (0, 0))],
        out_specs=pl.BlockSpec((None, s, gw), lambda i, g: (i, 0, g)),
        out_shape=jax.ShapeDtypeStruct((b, s, d_in), f32),
        scratch_shapes=[pltpu.VMEM((SSD_STATE, gw), f32)],
        compiler_params=_cp("parallel", "parallel"), name="ssd_scan",
    )(xbc3, xbc3, xbc3, p3, bias_v, a_v)


def _ssd_finish_kernel(y_ref, x_ref, z_ref, dsk_ref, g_ref, o_ref):
    y = y_ref[...] + x_ref[...].astype(f32) * dsk_ref[...]
    o_ref[...] = _rms(y * _silu(z_ref[...].astype(f32)), g_ref[...]).astype(o_ref.dtype)


def _ssd_finish(y, xbc, p, d_skip_row, norm_g):
    t, n = y.shape
    tm = ROW_TILE
    zb = _DST['ssd_z'] // n
    return pl.pallas_call(
        _ssd_finish_kernel, grid=(t // tm,),
        in_specs=[pl.BlockSpec((tm, n), lambda i: (i, 0)),
                  pl.BlockSpec((tm, n), lambda i: (i, 0)),
                  pl.BlockSpec((tm, n), lambda i: (i, zb)),
                  pl.BlockSpec((1, n), lambda i: (0, 0)),
                  pl.BlockSpec((1, n), lambda i: (0, 0))],
        out_specs=pl.BlockSpec((tm, n), lambda i: (i, 0)),
        out_shape=jax.ShapeDtypeStruct((t, n), bf16),
        compiler_params=_cp("parallel"), name="ssd_finish",
    )(y, xbc, p, d_skip_row, norm_g.reshape(1, n))


def _rope(x, cos, sin):
    lane = lax.broadcasted_iota(jnp.int32, (1, LANE), 1)
    first = (lane % 64) < 32
    partner = jnp.where(first, pltpu.roll(x, LANE - 32, 1), pltpu.roll(x, 32, 1))
    return x * cos + partner * sin


def _swa_kernel(sink_ref, q_ref, k_ref, v_ref, cos_ref, sin_ref, o_ref, *, LC, LL, j0):
    w = WINDOW
    hd = SWA_HEAD_DIM
    grp = SWA_HEADS // SWA_KV_HEADS
    kv_w = SWA_KV_HEADS * hd
    jb = pl.program_id(1) + j0
    n_ctx = LC // w
    is_lat = jb >= n_ctx
    nb = jnp.maximum(jb - n_ctx, 0)
    qs = pl.multiple_of(jb * w, w)
    k0 = pl.multiple_of(jnp.clip((nb - 1) * w, 0, LL - 3 * w), w)
    ks = pl.multiple_of(LC + k0, w)
    scale = hd ** -0.5

    cos_q, sin_q = cos_ref[pl.ds(qs, w), :], sin_ref[pl.ds(qs, w), :]
    cos_k, sin_k = cos_ref[pl.ds(ks, 3 * w), :], sin_ref[pl.ds(ks, 3 * w), :]
    q = q_ref[...].astype(f32)
    q = jnp.concatenate([_rope(q[:, c * LANE:(c + 1) * LANE], cos_q, sin_q) for c in range(SWA_HEADS * hd // LANE)],
                        axis=1) * scale
    kb = k_ref[pl.ds(ks, 3 * w), :].astype(f32)
    kb = jnp.concatenate([_rope(kb[:, c * LANE:(c + 1) * LANE], cos_k, sin_k) for c in range(kv_w // LANE)],
                         axis=1).astype(bf16)
    vb = v_ref[pl.ds(ks, 3 * w), :]
    kc = k_ref[0:LC, :]
    vc = v_ref[0:LC, :]

    rows = grp * w
    q_abs = nb * w + lax.broadcasted_iota(jnp.int32, (rows, 1), 0) % w
    k_abs = k0 + lax.broadcasted_iota(jnp.int32, (1, 3 * w), 1)
    valid = is_lat & (jnp.abs(k_abs - q_abs) <= w)
    head_of_row = lax.broadcasted_iota(jnp.int32, (rows, 1), 0) // w

    outs = []
    for g in range(SWA_KV_HEADS):
        qg = jnp.concatenate([q[:, (g * grp + r) * hd:(g * grp + r + 1) * hd] for r in range(grp)], axis=0).astype(bf16)
        sl = slice(g * hd, (g + 1) * hd)
        s_lat = jnp.where(valid, _dot_nt(qg, kb[:, sl]), -1e30)
        s_ctx = _dot_nt(qg, kc[:, sl])
        sink = jnp.zeros((rows, 1), f32)
        for r in range(grp):
            sink = jnp.where(head_of_row == r, sink_ref[g * grp + r], sink)
        m = jnp.maximum(jnp.maximum(jnp.max(s_lat, axis=1, keepdims=True), jnp.max(s_ctx, axis=1, keepdims=True)), sink)
        e_lat = jnp.exp(s_lat - m)
        e_ctx = jnp.exp(s_ctx - m)
        den = jnp.sum(e_lat, axis=1, keepdims=True) + jnp.sum(e_ctx, axis=1, keepdims=True) + jnp.exp(sink - m)
        og = (_dot(e_lat.astype(bf16), vb[:, sl]) + _dot(e_ctx.astype(bf16), vc[:, sl])) / den
        outs.extend(og[r * w:(r + 1) * w, :] for r in range(grp))
    o_ref[...] = jnp.concatenate(outs, axis=1).astype(o_ref.dtype)


def _swa(p3, sink, cos_s, sin_s, LC, LL, lat_only):
    b, s, _ = p3.shape
    w = WINDOW
    j0 = LC // w if lat_only else 0
    nq = s // w - j0
    qw = SWA_HEADS * SWA_HEAD_DIM
    kw = SWA_KV_HEADS * SWA_HEAD_DIM
    return pl.pallas_call(
        functools.partial(_swa_kernel, LC=LC, LL=LL, j0=j0), grid=(b, nq),
        in_specs=[pl.BlockSpec(memory_space=pltpu.SMEM),
                  pl.BlockSpec((None, w, qw), lambda i, j: (i, j + j0, _DST['swa_q'] // qw)),
                  pl.BlockSpec((None, s, kw), lambda i, j: (i, 0, _DST['swa_k'] // kw)),
                  pl.BlockSpec((None, s, kw), lambda i, j: (i, 0, _DST['swa_v'] // kw)),
                  pl.BlockSpec((s, LANE), lambda i, j: (0, 0)),
                  pl.BlockSpec((s, LANE), lambda i, j: (0, 0))],
        out_specs=pl.BlockSpec((None, w, qw), lambda i, j: (i, j + j0, 0)),
        out_shape=jax.ShapeDtypeStruct((b, s, qw), bf16),
        compiler_params=_cp("parallel", "parallel"), name="swa",
    )(sink, p3, p3, p3, cos_s, sin_s)


def _hgrn_kernel(q_ref, ff_ref, fr_ref, v_ref, g_ref, lb_ref, ng_ref, o_ref, acc_ref, st_ref, *, LC, LL):
    q = HG_CHUNK
    n_ctx, n_all = LC // q, (LC + LL) // q
    ii = lax.broadcasted_iota(jnp.int32, (q, q), 0)
    jj = lax.broadcasted_iota(jnp.int32, (q, q), 1)
    r8 = lax.broadcasted_iota(jnp.int32, (q, 1), 0) % HG_DIAG
    levels = []
    sz = HG_DIAG
    while sz < q:
        levels.append(sz)
        sz *= 2

    for d in range(2):
        rev = d == 1
        tri_f = ((ii <= jj) if rev else (ii >= jj)).astype(f32)
        last = 0 if rev else q - 1
        f_ref = fr_ref if rev else ff_ref
        st_ref[...] = jnp.zeros_like(st_ref)
        masks = []
        for sz in levels:
            same = (ii // (2 * sz)) == (jj // (2 * sz))
            i_hi, j_hi = (ii % (2 * sz)) >= sz, (jj % (2 * sz)) >= sz
            masks.append(same & (~i_hi & j_hi if rev else i_hi & ~j_hi))

        def chunk(c, carry, d=d, rev=rev, tri_f=tri_f, last=last, f_ref=f_ref, masks=masks):
            s = _chunk_start(c, n_ctx, n_all, q, rev)
            lb = lb_ref[d:d + 1, :]
            qq = _silu(q_ref[pl.ds(s, q), :].astype(f32))
            f = lb + (1.0 - lb) * _sigmoid(f_ref[pl.ds(s, q), :].astype(f32))
            kk = 1.0 - f
            v = v_ref[pl.ds(s, q), :].astype(f32)
            cum = _dot_exact(tri_f, jnp.log(f))
            cl = cum[last:last + 1, :]
            st = st_ref[...]
            o = _dot_nt((qq * jnp.exp(cum)).astype(bf16), st.astype(bf16))
            att = jnp.zeros((q, q), f32)
            for sz, mask in zip(levels, masks):
                pieces = []
                for b0 in range(0, q, 2 * sz):
                    rr = b0 + sz if rev else b0 + sz - 1
                    pieces.append(jnp.broadcast_to(cum[rr:rr + 1, :], (2 * sz, cum.shape[1])))
                e = jnp.exp(-jnp.abs(cum - jnp.concatenate(pieces, axis=0)))
                att = att + jnp.where(mask, _dot_nt((qq * e).astype(bf16), (kk * e).astype(bf16)), 0.0)
            o = o + _dot(att.astype(bf16), v.astype(bf16))
            o = o + jnp.sum(qq * kk, axis=1, keepdims=True) * v
            for dd in range(1, HG_DIAG):
                sh = q - dd if rev else dd
                ks, cs, vs = pltpu.roll(kk, sh, 0), pltpu.roll(cum, sh, 0), pltpu.roll(v, sh, 0)
                ok = (r8 <= HG_DIAG - 1 - dd) if rev else (r8 >= dd)
                pr = jnp.where(ok, qq * ks * jnp.exp(cum - cs), 0.0)
                o = o + jnp.sum(pr, axis=1, keepdims=True) * vs
            st_ref[...] = st * jnp.exp(cl) + _dot_tn(v.astype(bf16), (kk * jnp.exp(cl - cum)).astype(bf16))
            if rev:
                y = _rms(acc_ref[pl.ds(s, q), :] + o, ng_ref[...])
                o_ref[pl.ds(s, q), :] = (y * _sigmoid(g_ref[pl.ds(s, q), :].astype(f32))).astype(o_ref.dtype)
            else:
                acc_ref[pl.ds(s, q), :] = o
            return carry

        lax.fori_loop(0, n_all, chunk, 0)


def _hgrn(p3, lb, norm_g, LC, LL):
    b, s, _ = p3.shape
    dk = HG_KEY_DIM
    n = HG_HEADS * dk

    def col(name, extra=0):
        base = _DST[name] // dk + extra
        return pl.BlockSpec((None, s, dk), lambda i, h: (i, 0, base + h))

    return pl.pallas_call(
        functools.partial(_hgrn_kernel, LC=LC, LL=LL), grid=(b, HG_HEADS),
        in_specs=[col('hg_q'), col('hg_f'), col('hg_f', HG_HEADS), col('hg_i'), col('hg_g'),
                  pl.BlockSpec((2, dk), lambda i, h: (0, h)),
                  pl.BlockSpec((1, dk), lambda i, h: (0, h))],
        out_specs=pl.BlockSpec((None, s, dk), lambda i, h: (i, 0, h)),
        out_shape=jax.ShapeDtypeStruct((b, s, n), bf16),
        scratch_shapes=[pltpu.VMEM((s, dk), f32), pltpu.VMEM((HG_VAL_DIM, dk), f32)],
        compiler_params=_cp("parallel", "parallel"), name="hgrn",
    )(p3, p3, p3, p3, p3, lb, norm_g.reshape(1, n))


def _mla_prep_kernel(cq_ref, ckv_ref, misc_ref, qg_ref, kvg_ref, wq_ref, wkv_ref, cos_ref, sin_ref,
                     q_out, k_out, v_out):
    h = MLA_HEADS
    scale = (MLA_NOPE + MLA_ROPE) ** -0.5
    cos, sin = cos_ref[...], sin_ref[...]
    qf = _dot(_rms(cq_ref[...].astype(f32), qg_ref[...]).astype(bf16), wq_ref[...]) * scale
    kv = _dot(_rms(ckv_ref[...].astype(f32), kvg_ref[...]).astype(bf16), wkv_ref[...])
    kr = _rope(misc_ref[...].astype(f32), cos, sin)[:, 0:MLA_ROPE].astype(bf16)
    ro = h * MLA_NOPE
    for pp in range(h * MLA_ROPE // LANE):
        qr = _rope(qf[:, ro + pp * LANE:ro + (pp + 1) * LANE], cos, sin).astype(bf16)
        for e in range(LANE // MLA_ROPE):
            q_out[pp * (LANE // MLA_ROPE) + e, :, MLA_NOPE:MLA_NOPE + MLA_ROPE] = qr[:, e * MLA_ROPE:(e + 1) * MLA_ROPE]
    for hh in range(h):
        q_out[hh, :, 0:MLA_NOPE] = qf[:, hh * MLA_NOPE:(hh + 1) * MLA_NOPE].astype(bf16)
        k_out[hh, :, 0:MLA_NOPE] = kv[:, hh * MLA_NOPE:(hh + 1) * MLA_NOPE].astype(bf16)
        k_out[hh, :, MLA_NOPE:MLA_NOPE + MLA_ROPE] = kr
        v_out[hh] = kv[:, ro + hh * MLA_V:ro + (hh + 1) * MLA_V].astype(bf16)


def _mla_prep(p, q_norm_g, kv_norm_g, w_uq_p, w_ukv_p, cos_s, sin_s, S):
    t = p.shape[0]
    tm = ROW_TILE
    ns = S // tm
    h, qd = MLA_HEADS, MLA_NOPE + MLA_ROPE
    return pl.pallas_call(
        _mla_prep_kernel, grid=(t // tm,),
        in_specs=[pl.BlockSpec((tm, MLA_Q_RANK), lambda i: (i, _DST['mla_cq'] // MLA_Q_RANK)),
                  pl.BlockSpec((tm, MLA_KV_RANK), lambda i: (i, _DST['mla_ckv'] // MLA_KV_RANK)),
                  pl.BlockSpec((tm, LANE), lambda i: (i, _MISC // LANE)),
                  pl.BlockSpec((1, MLA_Q_RANK), lambda i: (0, 0)),
                  pl.BlockSpec((1, MLA_KV_RANK), lambda i: (0, 0)),
                  pl.BlockSpec(w_uq_p.shape, lambda i: (0, 0)),
                  pl.BlockSpec(w_ukv_p.shape, lambda i: (0, 0)),
                  pl.BlockSpec((tm, LANE), lambda i: (i % ns, 0)),
                  pl.BlockSpec((tm, LANE), lambda i: (i % ns, 0))],
        out_specs=[pl.BlockSpec((h, tm, qd), lambda i: (0, i, 0)),
                   pl.BlockSpec((h, tm, qd), lambda i: (0, i, 0)),
                   pl.BlockSpec((h, tm, MLA_V), lambda i: (0, i, 0))],
        out_shape=[jax.ShapeDtypeStruct((h, t, qd), bf16), jax.ShapeDtypeStruct((h, t, qd), bf16),
                   jax.ShapeDtypeStruct((h, t, MLA_V), bf16)],
        compiler_params=_cp("parallel"), name="mla_prep",
    )(p, p, p, q_norm_g.reshape(1, -1), kv_norm_g.reshape(1, -1), w_uq_p, w_ukv_p, cos_s, sin_s)


def _mla_attn_kernel(q_ref, k_ref, v_ref, o_ref, *, LC, j0, tq):
    jb = pl.program_id(2) + j0
    s = _dot_nt(q_ref[...], k_ref[...])
    key = lax.broadcasted_iota(jnp.int32, (1, s.shape[1]), 1)
    s = jnp.where((jb < LC // tq) & (key >= LC), -1e30, s)
    e = jnp.exp(s - jnp.max(s, axis=1, keepdims=True))
    o = _dot(e.astype(bf16), v_ref[...]) / jnp.sum(e, axis=1, keepdims=True)
    o_ref[...] = o.astype(o_ref.dtype)


def _mla_attn(qh, kh, vh, B, LC, LL, lat_only):
    h, t, qd = qh.shape
    s = LC + LL
    tq = ROW_TILE
    j0 = LC // tq if lat_only else 0
    nq = s // tq - j0
    ns = s // tq
    return pl.pallas_call(
        functools.partial(_mla_attn_kernel, LC=LC, j0=j0, tq=tq), grid=(B, h, nq),
        in_specs=[pl.BlockSpec((None, tq, qd), lambda b, hh, j: (hh, b * ns + j + j0, 0)),
                  pl.BlockSpec((None, s, qd), lambda b, hh, j: (hh, b, 0)),
                  pl.BlockSpec((None, s, MLA_V), lambda b, hh, j: (hh, b, 0))],
        out_specs=pl.BlockSpec((tq, MLA_V), lambda b, hh, j: (b * ns + j + j0, hh)),
        out_shape=jax.ShapeDtypeStruct((t, h * MLA_V), bf16),
        compiler_params=_cp("parallel", "parallel", "parallel"), name="mla_attn",
    )(qh, kh, vh)


def _merge_kernel(y0, y1, y2, y3, g0, g1, g2, g3, w_ref, o_ref):
    acc = None
    for n, (y_ref, g_ref) in enumerate(((y0, g0), (y1, g1), (y2, g2), (y3, g3))):
        t = _sigmoid(g_ref[...].astype(f32)) * _dot(y_ref[...], w_ref[n])
        acc = t if acc is None else acc + t
    o_ref[...] = acc.astype(o_ref.dtype)


def _merge(ys, p, w_branch_b, rows):
    t = p.shape[0]
    d = D_MODEL
    tm, tn = rows.tm, 512
    gb = _DST['gates'] // tn
    y_specs = [pl.BlockSpec((tm, BRANCH_W), lambda i, j: (rows.src(i), 0)) for _ in range(N_BRANCH)]
    g_specs = [pl.BlockSpec((tm, tn), functools.partial(lambda i, j, n: (rows.src(i), gb + n * (d // tn) + j), n=n))
               for n in range(N_BRANCH)]
    return pl.pallas_call(
        _merge_kernel, grid=(rows.n, d // tn),
        in_specs=y_specs + g_specs + [pl.BlockSpec((N_BRANCH, BRANCH_W, tn), lambda i, j: (0, 0, j))],
        out_specs=pl.BlockSpec((tm, tn), lambda i, j: (i, j)),
        out_shape=jax.ShapeDtypeStruct((rows.n * tm, d), bf16),
        compiler_params=_cp("parallel", "parallel"), name="merge",
    )(*ys, p, p, p, p, w_branch_b)


def _out_proj_kernel(m_ref, w_ref, x_ref, mod_ref, gpost_ref, gffn_ref, rw_ref, x_out, h_out, l_out):
    r = _rms(_dot(m_ref[...], w_ref[...]), gpost_ref[...])
    x = x_ref[...] + mod_ref[2:3, :] * r
    x_out[...] = x
    h2 = _rms(x, gffn_ref[...]) * (1.0 + mod_ref[4:5, :]) + mod_ref[3:4, :]
    h_out[...] = h2.astype(h_out.dtype)
    l_out[...] = _dot_exact(h2, rw_ref[...])


def _out_proj(m, w_out_b, x, mods, g_post, g_ffn, router_w, rows):
    d = D_MODEL
    tm = rows.tm
    n = rows.n * tm
    ne = router_w.shape[1]
    return pl.pallas_call(
        _out_proj_kernel, grid=(rows.n,),
        in_specs=[pl.BlockSpec((tm, d), lambda i: (i, 0)),
                  pl.BlockSpec((d, d), lambda i: (0, 0)),
                  pl.BlockSpec((tm, d), lambda i: (rows.src(i), 0)),
                  pl.BlockSpec((None, 6, d), lambda i: (rows.cond(i), 0, 0)),
                  pl.BlockSpec((1, d), lambda i: (0, 0)),
                  pl.BlockSpec((1, d), lambda i: (0, 0)),
                  pl.BlockSpec((d, ne), lambda i: (0, 0))],
        out_specs=[pl.BlockSpec((tm, d), lambda i: (i, 0)),
                   pl.BlockSpec((tm, d), lambda i: (i, 0)),
                   pl.BlockSpec((tm, ne), lambda i: (i, 0))],
        out_shape=[jax.ShapeDtypeStruct((n, d), f32), jax.ShapeDtypeStruct((n, d), bf16),
                   jax.ShapeDtypeStruct((n, ne), f32)],
        compiler_params=_cp("parallel"), name="out_proj",
    )(m, w_out_b, x, mods, g_post.reshape(1, d), g_ffn.reshape(1, d), router_w)


def _expert_kernel(be_ref, nu_ref, x_ref, w_ref, wg_ref, wu_ref, wd_ref, o_ref):
    i = pl.program_id(0)

    @pl.when(i < nu_ref[0])
    def _():
        x = x_ref[...]
        hmid = _silu(_dot(x, wg_ref[...])) * _dot(x, wu_ref[...])
        o_ref[...] = (_dot(hmid.astype(bf16), wd_ref[...]) * w_ref[...]).astype(o_ref.dtype)

    @pl.when(i >= nu_ref[0])
    def _():
        o_ref[...] = jnp.zeros_like(o_ref)


def _experts(x_sorted, row_w, blk_e, n_used, wg, wu, wd):
    n_rows, d = x_sorted.shape
    ff = wg.shape[2]
    bm = MOE_ROWS
    grid_spec = pltpu.PrefetchScalarGridSpec(
        num_scalar_prefetch=2, grid=(n_rows // bm,),
        in_specs=[pl.BlockSpec((bm, d), lambda i, be, nu: (i, 0)),
                  pl.BlockSpec((bm, 1), lambda i, be, nu: (i, 0)),
                  pl.BlockSpec((None, d, ff), lambda i, be, nu: (be[i], 0, 0)),
                  pl.BlockSpec((None, d, ff), lambda i, be, nu: (be[i], 0, 0)),
                  pl.BlockSpec((None, ff, d), lambda i, be, nu: (be[i], 0, 0))],
        out_specs=pl.BlockSpec((bm, d), lambda i, be, nu: (i, 0)))
    return pl.pallas_call(
        _expert_kernel, grid_spec=grid_spec,
        out_shape=jax.ShapeDtypeStruct((n_rows, d), bf16),
        compiler_params=_cp("arbitrary"), name="experts",
    )(blk_e, n_used, x_sorted, row_w, wg, wu, wd)


def _ffn_out_kernel(h_ref, r_ref, x_ref, mod_ref, g_ref, sg_ref, su_ref, sd_ref, o_ref):
    h = h_ref[...]
    mid = _silu(_dot(h, sg_ref[...])) * _dot(h, su_ref[...])
    f = _dot(mid.astype(bf16), sd_ref[...]) + r_ref[...]
    o_ref[...] = x_ref[...] + mod_ref[5:6, :] * _rms(f, g_ref[...])


def _ffn_out(h2, routed, x, mods, g_post, sg, su, sd, rows):
    n, d = h2.shape
    tm = rows.tm
    ff = sg.shape[1]
    return pl.pallas_call(
        _ffn_out_kernel, grid=(rows.n,),
        in_specs=[pl.BlockSpec((tm, d), lambda i: (i, 0)),
                  pl.BlockSpec((tm, d), lambda i: (i, 0)),
                  pl.BlockSpec((tm, d), lambda i: (i, 0)),
                  pl.BlockSpec((None, 6, d), lambda i: (rows.cond(i), 0, 0)),
                  pl.BlockSpec((1, d), lambda i: (0, 0)),
                  pl.BlockSpec((d, ff), lambda i: (0, 0)),
                  pl.BlockSpec((d, ff), lambda i: (0, 0)),
                  pl.BlockSpec((ff, d), lambda i: (0, 0))],
        out_specs=pl.BlockSpec((tm, d), lambda i: (i, 0)),
        out_shape=jax.ShapeDtypeStruct((n, d), f32),
        compiler_params=_cp("parallel"), name="ffn_out",
    )(h2, routed, x, mods, g_post.reshape(1, d), sg, su, sd)


def _route_kernel(l_ref, b_ref, e_out, w_out):
    scores = _sigmoid(l_ref[...])
    sel = scores + b_ref[...]
    tm, ne = sel.shape
    per_group = ne // N_EXPERT_GROUPS
    lane = lax.broadcasted_iota(jnp.int32, (tm, ne), 1)
    lane_f = lane.astype(f32)
    grp = lane // per_group
    neg = -jnp.inf

    def first_max(v):
        m = jnp.max(v, axis=1, keepdims=True)
        return m, jnp.min(jnp.where(v == m, lane_f, float(ne)), axis=1, keepdims=True)

    gs = []
    for g in range(N_EXPERT_GROUPS):
        sg = jnp.where(grp == g, sel, neg)
        m1, i1 = first_max(sg)
        gs.append(m1 + jnp.max(jnp.where(lane_f == i1, neg, sg), axis=1, keepdims=True))
    chosen = [jnp.zeros((tm, 1), jnp.bool_)] * N_EXPERT_GROUPS
    for _ in range(TOPK_EXPERT_GROUPS):
        m = functools.reduce(jnp.maximum, gs)
        found = jnp.zeros((tm, 1), jnp.bool_)
        for g in range(N_EXPERT_GROUPS):
            hit = (gs[g] == m) & ~found
            chosen[g] = chosen[g] | hit
            found = found | hit
            gs[g] = jnp.where(hit, neg, gs[g])
    keep = jnp.zeros((tm, ne), jnp.bool_)
    for g in range(N_EXPERT_GROUPS):
        keep = keep | ((grp == g) & chosen[g])
    cur = jnp.where(keep, sel, neg)
    ids, wts = [], []
    for _ in range(TOP_K):
        _, idx = first_max(cur)
        hit = lane_f == idx
        ids.append(idx)
        wts.append(jnp.sum(jnp.where(hit, scores, 0.0), axis=1, keepdims=True))
        cur = jnp.where(hit, neg, cur)
    norm = ROUTE_SCALE / functools.reduce(lambda a, b: a + b, wts)
    out_lane = lax.broadcasted_iota(jnp.int32, (tm, LANE), 1)
    e = jnp.zeros((tm, LANE), f32)
    w = jnp.zeros((tm, LANE), f32)
    for k in range(TOP_K):
        e = jnp.where(out_lane == k, ids[k], e)
        w = jnp.where(out_lane == k, wts[k] * norm, w)
    e_out[...] = e.astype(jnp.int32)
    w_out[...] = w


def _route(logits, router_bias):
    t, ne = logits.shape
    tm = ROW_TILE
    e, w = pl.pallas_call(
        _route_kernel, grid=(t // tm,),
        in_specs=[pl.BlockSpec((tm, ne), lambda i: (i, 0)), pl.BlockSpec((1, ne), lambda i: (0, 0))],
        out_specs=[pl.BlockSpec((tm, LANE), lambda i: (i, 0)), pl.BlockSpec((tm, LANE), lambda i: (i, 0))],
        out_shape=[jax.ShapeDtypeStruct((t, LANE), jnp.int32), jax.ShapeDtypeStruct((t, LANE), f32)],
        compiler_params=_cp("parallel"), name="route",
    )(logits, router_bias.astype(f32).reshape(1, ne))
    return e[:, :TOP_K], w[:, :TOP_K]


def _moe(h2, logits, router_bias, wg, wu, wd):
    t, d = h2.shape
    top_e, top_w = _route(logits, router_bias)
    n_assign = t * TOP_K
    flat_e = top_e.reshape(n_assign)
    onehot = (flat_e[:, None] == jnp.arange(N_EXPERTS)[None, :]).astype(jnp.int32)
    csum = jnp.cumsum(onehot, axis=0)
    counts = csum[-1]
    rank = jnp.take_along_axis(csum, flat_e[:, None], axis=1)[:, 0] - 1
    padded = (counts + MOE_ROWS - 1) // MOE_ROWS * MOE_ROWS
    pad_end = jnp.cumsum(padded)
    dest = (pad_end - padded)[flat_e] + rank
    n_blocks = -(-n_assign // MOE_ROWS) + N_EXPERTS
    n_rows = n_blocks * MOE_ROWS
    row_tok = jnp.zeros((n_rows,), jnp.int32).at[dest].set(jnp.arange(n_assign, dtype=jnp.int32) // TOP_K)
    row_w = jnp.zeros((n_rows,), f32).at[dest].set(top_w.reshape(n_assign))
    blk_e = jnp.minimum(jnp.searchsorted(pad_end, jnp.arange(n_blocks) * MOE_ROWS, side='right'),
                        N_EXPERTS - 1).astype(jnp.int32)
    n_used = (pad_end[-1] // MOE_ROWS).astype(jnp.int32).reshape(1)
    x_sorted = jnp.take(h2, row_tok, axis=0)
    y_sorted = _experts(x_sorted, row_w.reshape(n_rows, 1), blk_e, n_used, wg, wu, wd)
    return jnp.take(y_sorted, dest, axis=0).reshape(t, TOP_K, d).astype(f32).sum(axis=1)


def _permute_w_in(w_in):
    cols = [w_in[:, _SRC[name][0]:_SRC[name][0] + _SRC[name][1]] for name in _P_ORDER]
    cols.append(jnp.zeros((w_in.shape[0], _P_WIDTH - _P_USED), w_in.dtype))
    return jnp.concatenate(cols, axis=1).astype(bf16)


def _permute_mla(w_uq, w_ukv):
    h = MLA_HEADS
    q = w_uq.reshape(-1, h, MLA_NOPE + MLA_ROPE)
    w_uq_p = jnp.concatenate([q[:, :, :MLA_NOPE].reshape(-1, h * MLA_NOPE), q[:, :, MLA_NOPE:].reshape(-1, h * MLA_ROPE)], 1)
    kv = w_ukv.reshape(-1, h, MLA_NOPE + MLA_V)
    w_ukv_p = jnp.concatenate([kv[:, :, :MLA_NOPE].reshape(-1, h * MLA_NOPE), kv[:, :, MLA_NOPE:].reshape(-1, h * MLA_V)], 1)
    return w_uq_p.astype(bf16), w_ukv_p.astype(bf16)


def _rope_tables(LC, LL):
    pos = jnp.arange(LL)
    n_freq = SWA_HEAD_DIM // 4
    inv_freq = ROPE_BASE ** (-jnp.arange(n_freq, dtype=f32) / n_freq)
    ang = jnp.concatenate([(pos // GRID_W).astype(f32)[:, None] * inv_freq,
                           (pos % GRID_W).astype(f32)[:, None] * inv_freq], axis=-1)
    cos, sin = jnp.cos(ang), jnp.sin(ang)
    cos = jnp.concatenate([jnp.ones((LC, cos.shape[1]), f32), cos], axis=0)
    sin = jnp.concatenate([jnp.zeros((LC, sin.shape[1]), f32), sin], axis=0)
    return jnp.tile(jnp.concatenate([cos, cos], 1), (1, 2)), jnp.tile(jnp.concatenate([-sin, sin], 1), (1, 2))


def _dir_lanes(v):
    out = jnp.zeros((2, LANE), f32)
    for d in range(2):
        out = out.at[d, _DT_LANE + d * SSD_HEADS:_DT_LANE + (d + 1) * SSD_HEADS].set(v[d].astype(f32))
    return out


def kernel(x, c, ctx, c_ctx, w_mod, b_mod, norm_pre_mix, norm_post_mix, norm_pre_ffn, norm_post_ffn, w_in, ssd_conv_w, ssd_conv_b, ssd_dt_bias, ssd_a_log, ssd_d, ssd_norm, swa_sink, hg_lb_logits, hg_norm, mla_q_norm, mla_kv_norm, mla_w_uq, mla_w_ukv, w_branch, w_out, router_w, router_bias, expert_w_gate, expert_w_up, expert_w_down, shared_w_gate, shared_w_up, shared_w_down):
    B, LL, d = x.shape
    LC = ctx.shape[1]
    S = LC + LL
    depth = w_mod.shape[0]
    assert LC % ROW_TILE == 0 and LL % ROW_TILE == 0 and LL >= 3 * WINDOW and d == D_MODEL

    n_cond = -(-(B + 1) // 8) * 8
    cond = jnp.zeros((n_cond, d), f32).at[:B].set(c).at[B].set(c_ctx)
    mods_all = _modulation(cond, w_mod, b_mod).reshape(depth, n_cond, 6, d)
    cos_s, sin_s = _rope_tables(LC, LL)
    sm = jax.nn.softmax(hg_lb_logits.astype(f32), axis=0)
    lower_bounds = jnp.cumsum(sm, axis=0) - sm[0]

    xs = jnp.concatenate([ctx, x], axis=1).reshape(B * S, d)
    for l in range(depth):
        last = l == depth - 1
        mods = mods_all[l]
        all_rows = _Rows(B, LC, LL, False)
        out_rows = _Rows(B, LC, LL, last)
        h = _norm_mod(xs, norm_pre_mix[l], mods, all_rows)
        p = _matmul(h, _permute_w_in(w_in[l]), 1024, _P_TN, bf16, "in_proj")
        p3 = p.reshape(B, S, _P_WIDTH)

        xbc3 = _ssd_conv(p3, ssd_conv_w[l], ssd_conv_b[l], LC, LL)
        y_ssd = _ssd_scan(xbc3, p3, _dir_lanes(ssd_dt_bias[l]), _dir_lanes(-jnp.exp(ssd_a_log[l].astype(f32))), LC, LL)
        d_in = SSD_HEADS * SSD_HEAD_DIM
        y_ssd = _ssd_finish(y_ssd.reshape(B * S, d_in), xbc3.reshape(B * S, -1), p,
                            jnp.repeat(ssd_d[l].astype(f32), SSD_HEAD_DIM).reshape(1, d_in), ssd_norm[l])
        y_swa = _swa(p3, swa_sink[l].astype(f32), cos_s, sin_s, LC, LL, last).reshape(B * S, -1)
        y_hg = _hgrn(p3, lower_bounds[l], hg_norm[l], LC, LL).reshape(B * S, -1)
        w_uq_p, w_ukv_p = _permute_mla(mla_w_uq[l], mla_w_ukv[l])
        qh, kh, vh = _mla_prep(p, mla_q_norm[l], mla_kv_norm[l], w_uq_p, w_ukv_p, cos_s, sin_s, S)
        y_mla = _mla_attn(qh, kh, vh, B, LC, LL, last)

        m = _merge((y_ssd, y_swa, y_hg, y_mla), p, w_branch[l].astype(bf16), out_rows)
        x_mid, h2, logits = _out_proj(m, w_out[l].astype(bf16), xs, mods, norm_post_mix[l], norm_pre_ffn[l],
                                      router_w[l], out_rows)
        routed = _moe(h2, logits, router_bias[l], expert_w_gate[l].astype(bf16), expert_w_up[l].astype(bf16),
                      expert_w_down[l].astype(bf16))
        xs = _ffn_out(h2, routed, x_mid, mods, norm_post_ffn[l], shared_w_gate[l].astype(bf16),
                      shared_w_up[l].astype(bf16), shared_w_down[l].astype(bf16), out_rows)
    return xs.reshape(B, LL, d)
```

```python
import functools
import math

import jax
import jax.numpy as jnp
import numpy as np
from jax import lax
from jax.experimental import pallas as pl
from jax.experimental.pallas import tpu as pltpu

f32 = jnp.float32
bf16 = jnp.bfloat16

D_MODEL = 2048
GRID_W = 64
N_BRANCH = 4
BRANCH_W = 1024
EPS = 1e-6
ROPE_BASE = 10000.0

SSD_HEADS = 16
SSD_HEAD_DIM = 64
SSD_GROUPS = 4
SSD_STATE = 128
SSD_CONV = 5
SSD_CHUNK = 128

SWA_HEADS = 16
SWA_KV_HEADS = 4
SWA_HEAD_DIM = 64
WINDOW = 128

HG_HEADS = 8
HG_KEY_DIM = 128
HG_VAL_DIM = 128
HG_CHUNK = 64
HG_DIAG = 8

MLA_HEADS = 8
MLA_Q_RANK = 512
MLA_KV_RANK = 256
MLA_NOPE = 128
MLA_ROPE = 64
MLA_V = 128

N_EXPERTS = 64
TOP_K = 6
N_EXPERT_GROUPS = 8
TOPK_EXPERT_GROUPS = 4
EXPERT_FF = 512
SHARED_FF = 1024
ROUTE_SCALE = 2.5
MOE_ROWS = 256

LANE = 128
ROW_TILE = 256
VMEM_LIMIT = 48 * 1024 * 1024

_IN_LAYOUT = (
    ('ssd_z', 1024), ('ssd_x', 1024), ('ssd_B', 512), ('ssd_C', 512), ('ssd_dt', 32),
    ('swa_q', 1024), ('swa_k', 256), ('swa_v', 256),
    ('hg_q', 1024), ('hg_f', 2048), ('hg_i', 1024), ('hg_g', 1024),
    ('mla_cq', 512), ('mla_ckv', 256), ('mla_kr', 64), ('gates', 8192),
)
_P_ORDER = ('gates', 'hg_f', 'ssd_x', 'ssd_B', 'ssd_C', 'ssd_z', 'swa_q', 'hg_q', 'hg_i', 'hg_g',
            'mla_cq', 'swa_k', 'swa_v', 'mla_ckv', 'mla_kr', 'ssd_dt')
_P_TN = 512


def _layout():
    src, off = {}, 0
    for name, size in _IN_LAYOUT:
        src[name] = (off, size)
        off += size
    dst, off = {}, 0
    for name in _P_ORDER:
        dst[name] = off
        off += src[name][1]
    width = -(-off // _P_TN) * _P_TN
    return src, dst, off, width


_SRC, _DST, _P_USED, _P_WIDTH = _layout()
_MISC = _DST['mla_kr']
_DT_LANE = _DST['ssd_dt'] - _MISC


def _cp(*sem):
    return pltpu.CompilerParams(dimension_semantics=sem, vmem_limit_bytes=VMEM_LIMIT)


def _sigmoid(x):
    return 1.0 / (1.0 + jnp.exp(-x))


def _silu(x):
    return x * _sigmoid(x)


def _rms(x, gain):
    return x * lax.rsqrt(jnp.mean(x * x, axis=-1, keepdims=True) + EPS) * gain


def _dot(a, b):
    return jnp.dot(a, b, preferred_element_type=f32)


def _dot_nt(a, b):
    return lax.dot_general(a, b, (((1,), (1,)), ((), ())), preferred_element_type=f32)


def _dot_tn(a, b):
    return lax.dot_general(a, b, (((0,), (0,)), ((), ())), preferred_element_type=f32)


def _dot_exact(a, b):
    return jnp.dot(a, b, preferred_element_type=f32, precision=lax.Precision.HIGHEST)


class _Rows:
    def __init__(self, B, LC, LL, lat_only, tm=ROW_TILE):
        self.B, self.LC, self.LL, self.tm = B, LC, LL, tm
        self.nC, self.nL = LC // tm, LL // tm
        self.nS = self.nC + self.nL
        self.lat_only = lat_only
        self.n = B * (self.nL if lat_only else self.nS)

    def src(self, i):
        if self.lat_only:
            return (i // self.nL) * self.nS + self.nC + i % self.nL
        return i

    def cond(self, i):
        if self.lat_only:
            return i // self.nL
        return jnp.where(i % self.nS < self.nC, self.B, i // self.nS)

    def pos(self, i):
        if self.lat_only:
            return self.nC + i % self.nL
        return i % self.nS


def _mod_kernel(c_ref, w_ref, b_ref, o_ref):
    s = _silu(c_ref[...]).astype(bf16)
    o_ref[...] = _dot(s, w_ref[...].astype(bf16)) + b_ref[...]


def _modulation(cond, w_mod, b_mod):
    nl, d, n = w_mod.shape
    nc = cond.shape[0]
    tn = 1024
    return pl.pallas_call(
        _mod_kernel, grid=(nl, n // tn),
        in_specs=[pl.BlockSpec((nc, d), lambda l, j: (0, 0)),
                  pl.BlockSpec((None, d, tn), lambda l, j: (l, 0, j)),
                  pl.BlockSpec((None, 1, tn), lambda l, j: (l, 0, j))],
        out_specs=pl.BlockSpec((None, nc, tn), lambda l, j: (l, 0, j)),
        out_shape=jax.ShapeDtypeStruct((nl, nc, n), f32),
        compiler_params=_cp("parallel", "parallel"), name="modulation",
    )(cond, w_mod, b_mod.reshape(nl, 1, n))


def _norm_mod_kernel(x_ref, g_ref, mod_ref, o_ref):
    y = _rms(x_ref[...], g_ref[...])
    o_ref[...] = (y * (1.0 + mod_ref[1:2, :]) + mod_ref[0:1, :]).astype(o_ref.dtype)


def _norm_mod(x, gain, mods, rows):
    t, d = x.shape
    tm = rows.tm
    return pl.pallas_call(
        _norm_mod_kernel, grid=(rows.n,),
        in_specs=[pl.BlockSpec((tm, d), lambda i: (rows.src(i), 0)),
                  pl.BlockSpec((1, d), lambda i: (0, 0)),
                  pl.BlockSpec((None, 6, d), lambda i: (rows.cond(i), 0, 0))],
        out_specs=pl.BlockSpec((tm, d), lambda i: (rows.src(i), 0)),
        out_shape=jax.ShapeDtypeStruct((t, d), bf16),
        compiler_params=_cp("parallel"), name="norm_mod",
    )(x, gain.reshape(1, d), mods)


def _mm_kernel(a_ref, b_ref, o_ref):
    o_ref[...] = _dot(a_ref[...], b_ref[...]).astype(o_ref.dtype)


def _matmul(a, b, tm, tn, out_dtype, name):
    m, k = a.shape
    n = b.shape[1]
    return pl.pallas_call(
        _mm_kernel, grid=(m // tm, n // tn),
        in_specs=[pl.BlockSpec((tm, k), lambda i, j: (i, 0)),
                  pl.BlockSpec((k, tn), lambda i, j: (0, j))],
        out_specs=pl.BlockSpec((tm, tn), lambda i, j: (i, j)),
        out_shape=jax.ShapeDtypeStruct((m, n), out_dtype),
        compiler_params=_cp("parallel", "parallel"), name=name,
    )(a, b)


def _ssd_conv_kernel(x_ref, w_ref, b_ref, o_ref, *, LC, LL):
    s = LC + LL
    x = x_ref[...].astype(f32)
    row = lax.broadcasted_iota(jnp.int32, (s, 1), 0)
    pos = jnp.where(row < LC, row, row - LC)
    length = jnp.where(row < LC, LC, LL)
    acc = jnp.zeros_like(x) + b_ref[...]
    half = SSD_CONV // 2
    for k in range(SSD_CONV):
        sh = k - half
        xs = x if sh == 0 else pltpu.roll(x, (-sh) % s, 0)
        valid = (pos + sh >= 0) & (pos + sh < length)
        acc = acc + w_ref[k:k + 1, :] * jnp.where(valid, xs, 0.0)
    o_ref[...] = _silu(acc).astype(o_ref.dtype)


def _ssd_conv(p3, conv_w, conv_b, LC, LL):
    b, s, _ = p3.shape
    n = conv_w.shape[1]
    tc = 256
    base = _DST['ssd_x'] // tc
    return pl.pallas_call(
        functools.partial(_ssd_conv_kernel, LC=LC, LL=LL), grid=(b, n // tc),
        in_specs=[pl.BlockSpec((None, s, tc), lambda i, j: (i, 0, base + j)),
                  pl.BlockSpec((SSD_CONV, tc), lambda i, j: (0, j)),
                  pl.BlockSpec((1, tc), lambda i, j: (0, j))],
        out_specs=pl.BlockSpec((None, s, tc), lambda i, j: (i, 0, j)),
        out_shape=jax.ShapeDtypeStruct((b, s, n), bf16),
        compiler_params=_cp("parallel", "parallel"), name="ssd_conv",
    )(p3, conv_w, conv_b.reshape(1, n))


def _chunk_start(c, n_ctx, n_all, q, rev):
    if rev:
        c = jnp.where(c < n_ctx, n_ctx - 1 - c, n_all - 1 - (c - n_ctx))
    return pl.multiple_of(c * q, q)


def _pick4(hid, vals):
    return jnp.where(hid == 0, vals[0], jnp.where(hid == 1, vals[1], jnp.where(hid == 2, vals[2], vals[3])))


def _ssd_scan_kernel(x_ref, b_ref, c_ref, dt_ref, bias_ref, a_ref, o_ref, st_ref, *, LC, LL):
    q = SSD_CHUNK
    r = SSD_HEADS // SSD_GROUPS
    g = pl.program_id(1)
    n_ctx, n_all = LC // q, (LC + LL) // q
    ii = lax.broadcasted_iota(jnp.int32, (q, q), 0)
    jj = lax.broadcasted_iota(jnp.int32, (q, q), 1)
    lane = lax.broadcasted_iota(jnp.int32, (1, LANE), 1)
    sub = lax.broadcasted_iota(jnp.int32, (LANE, 1), 0)
    hid = lax.broadcasted_iota(jnp.int32, (1, r * SSD_HEAD_DIM), 1) // SSD_HEAD_DIM
    lane_lo = lane < SSD_HEAD_DIM

    for d in range(2):
        rev = d == 1
        tri = (ii <= jj) if rev else (ii >= jj)
        tri_f = tri.astype(f32)
        last = 0 if rev else q - 1
        st_ref[...] = jnp.zeros_like(st_ref)

        def chunk(c, carry, d=d, rev=rev, tri=tri, tri_f=tri_f, last=last):
            s = _chunk_start(c, n_ctx, n_all, q, rev)
            raw = dt_ref[pl.ds(s, q), :].astype(f32) + bias_ref[d:d + 1, :]
            dtv = jnp.maximum(raw, 0.0) + jnp.log1p(jnp.exp(-jnp.abs(raw)))
            a = dtv * a_ref[d:d + 1, :]
            acum = _dot_exact(tri_f, a)
            acum_t = acum.T
            x = x_ref[pl.ds(s, q), :].astype(f32)
            bm = b_ref[pl.ds(s, q), :]
            cm = c_ref[pl.ds(s, q), :]
            cb = _dot_nt(cm, bm)
            acol, arow, dcol, alast = [], [], [], []
            for hh in range(r):
                col = _DT_LANE + d * SSD_HEADS + g * r + hh
                acol.append(jnp.sum(jnp.where(lane == col, acum, 0.0), axis=1, keepdims=True))
                arow.append(jnp.sum(jnp.where(sub == col, acum_t, 0.0), axis=0, keepdims=True))
                dcol.append(jnp.sum(jnp.where(lane == col, dtv, 0.0), axis=1, keepdims=True))
                alast.append(acol[hh][last:last + 1, :])
            a_all = _pick4(hid, acol)
            al_all = _pick4(hid, alast)
            dtx = _pick4(hid, dcol) * x
            st = st_ref[...]
            y_off = _dot(cm, st.astype(bf16)) * jnp.exp(a_all)
            y_diag = []
            for pp in range(r // 2):
                decay = [jnp.exp(jnp.where(tri, acol[h] - arow[h], -1e30)) * cb for h in (2 * pp, 2 * pp + 1)]
                lhs = jnp.concatenate(decay, axis=1).astype(bf16)
                xp = dtx[:, pp * LANE:(pp + 1) * LANE]
                rhs = jnp.concatenate([jnp.where(lane_lo, xp, 0.0), jnp.where(lane_lo, 0.0, xp)], axis=0)
                y_diag.append(_dot(lhs, rhs.astype(bf16)))
            y = y_off + jnp.concatenate(y_diag, axis=1)
            if rev:
                o_ref[pl.ds(s, q), :] += y
            else:
                o_ref[pl.ds(s, q), :] = y
            w = (jnp.exp(al_all - a_all) * dtx).astype(bf16)
            st_ref[...] = st * jnp.exp(al_all) + _dot_tn(bm, w)
            return carry

        lax.fori_loop(0, n_all, chunk, 0, unroll=2)


def _ssd_scan(xbc3, p3, bias_v, a_v, LC, LL):
    b, s, _ = xbc3.shape
    d_in = SSD_HEADS * SSD_HEAD_DIM
    gw = d_in // SSD_GROUPS
    nb = d_in // SSD_STATE
    return pl.pallas_call(
        functools.partial(_ssd_scan_kernel, LC=LC, LL=LL), grid=(b, SSD_GROUPS),
        in_specs=[pl.BlockSpec((None, s, gw), lambda i, g: (i, 0, g)),
                  pl.BlockSpec((None, s, SSD_STATE), lambda i, g: (i, 0, nb + g)),
                  pl.BlockSpec((None, s, SSD_STATE), lambda i, g: (i, 0, nb + SSD_GROUPS + g)),
                  pl.BlockSpec((None, s, LANE), lambda i, g: (i, 0, _MISC // LANE)),
                  pl.BlockSpec((2, LANE), lambda i, g: (0, 0)),
                  pl.BlockSpec((2, LANE), lambda i, g: (0, 0))],
        out_specs=pl.BlockSpec((None, s, gw), lambda i, g: (i, 0, g)),
        out_shape=jax.ShapeDtypeStruct((b, s, d_in), f32),
        scratch_shapes=[pltpu.VMEM((SSD_STATE, gw), f32)],
        compiler_params=_cp("parallel", "parallel"), name="ssd_scan",
    )(xbc3, xbc3, xbc3, p3, bias_v, a_v)


def _ssd_finish_kernel(y_ref, x_ref, z_ref, dsk_ref, g_ref, o_ref):
    y = y_ref[...] + x_ref[...].astype(f32) * dsk_ref[...]
    o_ref[...] = _rms(y * _silu(z_ref[...].astype(f32)), g_ref[...]).astype(o_ref.dtype)


def _ssd_finish(y, xbc, p, d_skip_row, norm_g):
    t, n = y.shape
    tm = ROW_TILE
    zb = _DST['ssd_z'] // n
    return pl.pallas_call(
        _ssd_finish_kernel, grid=(t // tm,),
        in_specs=[pl.BlockSpec((tm, n), lambda i: (i, 0)),
                  pl.BlockSpec((tm, n), lambda i: (i, 0)),
                  pl.BlockSpec((tm, n), lambda i: (i, zb)),
                  pl.BlockSpec((1, n), lambda i: (0, 0)),
                  pl.BlockSpec((1, n), lambda i: (0, 0))],
        out_specs=pl.BlockSpec((tm, n), lambda i: (i, 0)),
        out_shape=jax.ShapeDtypeStruct((t, n), bf16),
        compiler_params=_cp("parallel"), name="ssd_finish",
    )(y, xbc, p, d_skip_row, norm_g.reshape(1, n))


def _rope(x, cos, sin):
    lane = lax.broadcasted_iota(jnp.int32, (1, LANE), 1)
    first = (lane % 64) < 32
    partner = jnp.where(first, pltpu.roll(x, LANE - 32, 1), pltpu.roll(x, 32, 1))
    return x * cos + partner * sin


def _swa_kernel(sink_ref, q_ref, k_ref, v_ref, cos_ref, sin_ref, o_ref, *, LC, LL, j0):
    w = WINDOW
    hd = SWA_HEAD_DIM
    grp = SWA_HEADS // SWA_KV_HEADS
    kv_w = SWA_KV_HEADS * hd
    jb = pl.program_id(1) + j0
    n_ctx = LC // w
    is_lat = jb >= n_ctx
    nb = jnp.maximum(jb - n_ctx, 0)
    qs = pl.multiple_of(jb * w, w)
    k0 = pl.multiple_of(jnp.clip((nb - 1) * w, 0, LL - 3 * w), w)
    ks = pl.multiple_of(LC + k0, w)
    scale = hd ** -0.5

    cos_q, sin_q = cos_ref[pl.ds(qs, w), :], sin_ref[pl.ds(qs, w), :]
    cos_k, sin_k = cos_ref[pl.ds(ks, 3 * w), :], sin_ref[pl.ds(ks, 3 * w), :]
    q = q_ref[...].astype(f32)
    q = jnp.concatenate([_rope(q[:, c * LANE:(c + 1) * LANE], cos_q, sin_q) for c in range(SWA_HEADS * hd // LANE)],
                        axis=1) * scale
    kb = k_ref[pl.ds(ks, 3 * w), :].astype(f32)
    kb = jnp.concatenate([_rope(kb[:, c * LANE:(c + 1) * LANE], cos_k, sin_k) for c in range(kv_w // LANE)],
                         axis=1).astype(bf16)
    vb = v_ref[pl.ds(ks, 3 * w), :]
    kc = k_ref[0:LC, :]
    vc = v_ref[0:LC, :]

    rows = grp * w
    q_abs = nb * w + lax.broadcasted_iota(jnp.int32, (rows, 1), 0) % w
    k_abs = k0 + lax.broadcasted_iota(jnp.int32, (1, 3 * w), 1)
    valid = is_lat & (jnp.abs(k_abs - q_abs) <= w)
    head_of_row = lax.broadcasted_iota(jnp.int32, (rows, 1), 0) // w

    outs = []
    for g in range(SWA_KV_HEADS):
        qg = jnp.concatenate([q[:, (g * grp + r) * hd:(g * grp + r + 1) * hd] for r in range(grp)], axis=0).astype(bf16)
        sl = slice(g * hd, (g + 1) * hd)
        s_lat = jnp.where(valid, _dot_nt(qg, kb[:, sl]), -1e30)
        s_ctx = _dot_nt(qg, kc[:, sl])
        sink = jnp.zeros((rows, 1), f32)
        for r in range(grp):
            sink = jnp.where(head_of_row == r, sink_ref[g * grp + r], sink)
        m = jnp.maximum(jnp.maximum(jnp.max(s_lat, axis=1, keepdims=True), jnp.max(s_ctx, axis=1, keepdims=True)), sink)
        e_lat = jnp.exp(s_lat - m)
        e_ctx = jnp.exp(s_ctx - m)
        den = jnp.sum(e_lat, axis=1, keepdims=True) + jnp.sum(e_ctx, axis=1, keepdims=True) + jnp.exp(sink - m)
        og = (_dot(e_lat.astype(bf16), vb[:, sl]) + _dot(e_ctx.astype(bf16), vc[:, sl])) / den
        outs.extend(og[r * w:(r + 1) * w, :] for r in range(grp))
    o_ref[...] = jnp.concatenate(outs, axis=1).astype(o_ref.dtype)


def _swa(p3, sink, cos_s, sin_s, LC, LL, lat_only):
    b, s, _ = p3.shape
    w = WINDOW
    j0 = LC // w if lat_only else 0
    nq = s // w - j0
    qw = SWA_HEADS * SWA_HEAD_DIM
    kw = SWA_KV_HEADS * SWA_HEAD_DIM
    return pl.pallas_call(
        functools.partial(_swa_kernel, LC=LC, LL=LL, j0=j0), grid=(b, nq),
        in_specs=[pl.BlockSpec(memory_space=pltpu.SMEM),
                  pl.BlockSpec((None, w, qw), lambda i, j: (i, j + j0, _DST['swa_q'] // qw)),
                  pl.BlockSpec((None, s, kw), lambda i, j: (i, 0, _DST['swa_k'] // kw)),
                  pl.BlockSpec((None, s, kw), lambda i, j: (i, 0, _DST['swa_v'] // kw)),
                  pl.BlockSpec((s, LANE), lambda i, j: (0, 0)),
                  pl.BlockSpec((s, LANE), lambda i, j: (0, 0))],
        out_specs=pl.BlockSpec((None, w, qw), lambda i, j: (i, j + j0, 0)),
        out_shape=jax.ShapeDtypeStruct((b, s, qw), bf16),
        compiler_params=_cp("parallel", "parallel"), name="swa",
    )(sink, p3, p3, p3, cos_s, sin_s)


def _hgrn_kernel(q_ref, ff_ref, fr_ref, v_ref, g_ref, lb_ref, ng_ref, o_ref, acc_ref, st_ref, *, LC, LL):
    q = HG_CHUNK
    n_ctx, n_all = LC // q, (LC + LL) // q
    ii = lax.broadcasted_iota(jnp.int32, (q, q), 0)
    jj = lax.broadcasted_iota(jnp.int32, (q, q), 1)
    r8 = lax.broadcasted_iota(jnp.int32, (q, 1), 0) % HG_DIAG
    levels = []
    sz = HG_DIAG
    while sz < q:
        levels.append(sz)
        sz *= 2

    for d in range(2):
        rev = d == 1
        tri_f = ((ii <= jj) if rev else (ii >= jj)).astype(f32)
        last = 0 if rev else q - 1
        f_ref = fr_ref if rev else ff_ref
        st_ref[...] = jnp.zeros_like(st_ref)
        masks = []
        for sz in levels:
            same = (ii // (2 * sz)) == (jj // (2 * sz))
            i_hi, j_hi = (ii % (2 * sz)) >= sz, (jj % (2 * sz)) >= sz
            masks.append(same & (~i_hi & j_hi if rev else i_hi & ~j_hi))

        def chunk(c, carry, d=d, rev=rev, tri_f=tri_f, last=last, f_ref=f_ref, masks=masks):
            s = _chunk_start(c, n_ctx, n_all, q, rev)
            lb = lb_ref[d:d + 1, :]
            qq = _silu(q_ref[pl.ds(s, q), :].astype(f32))
            f = lb + (1.0 - lb) * _sigmoid(f_ref[pl.ds(s, q), :].astype(f32))
            kk = 1.0 - f
            v = v_ref[pl.ds(s, q), :].astype(f32)
            cum = _dot_exact(tri_f, jnp.log(f))
            cl = cum[last:last + 1, :]
            st = st_ref[...]
            o = _dot_nt((qq * jnp.exp(cum)).astype(bf16), st.astype(bf16))
            att = jnp.zeros((q, q), f32)
            for sz, mask in zip(levels, masks):
                pieces = []
                for b0 in range(0, q, 2 * sz):
                    rr = b0 + sz if rev else b0 + sz - 1
                    pieces.append(jnp.broadcast_to(cum[rr:rr + 1, :], (2 * sz, cum.shape[1])))
                e = jnp.exp(-jnp.abs(cum - jnp.concatenate(pieces, axis=0)))
                att = att + jnp.where(mask, _dot_nt((qq * e).astype(bf16), (kk * e).astype(bf16)), 0.0)
            o = o + _dot(att.astype(bf16), v.astype(bf16))
            o = o + jnp.sum(qq * kk, axis=1, keepdims=True) * v
            for dd in range(1, HG_DIAG):
                sh = q - dd if rev else dd
                ks, cs, vs = pltpu.roll(kk, sh, 0), pltpu.roll(cum, sh, 0), pltpu.roll(v, sh, 0)
                ok = (r8 <= HG_DIAG - 1 - dd) if rev else (r8 >= dd)
                pr = jnp.where(ok, qq * ks * jnp.exp(cum - cs), 0.0)
                o = o + jnp.sum(pr, axis=1, keepdims=True) * vs
            st_ref[...] = st * jnp.exp(cl) + _dot_tn(v.astype(bf16), (kk * jnp.exp(cl - cum)).astype(bf16))
            if rev:
                y = _rms(acc_ref[pl.ds(s, q), :] + o, ng_ref[...])
                o_ref[pl.ds(s, q), :] = (y * _sigmoid(g_ref[pl.ds(s, q), :].astype(f32))).astype(o_ref.dtype)
            else:
                acc_ref[pl.ds(s, q), :] = o
            return carry

        lax.fori_loop(0, n_all, chunk, 0, unroll=2)


def _hgrn(p3, lb, norm_g, LC, LL):
    b, s, _ = p3.shape
    dk = HG_KEY_DIM
    n = HG_HEADS * dk

    def col(name, extra=0):
        base = _DST[name] // dk + extra
        return pl.BlockSpec((None, s, dk), lambda i, h: (i, 0, base + h))

    return pl.pallas_call(
        functools.partial(_hgrn_kernel, LC=LC, LL=LL), grid=(b, HG_HEADS),
        in_specs=[col('hg_q'), col('hg_f'), col('hg_f', HG_HEADS), col('hg_i'), col('hg_g'),
                  pl.BlockSpec((2, dk), lambda i, h: (0, h)),
                  pl.BlockSpec((1, dk), lambda i, h: (0, h))],
        out_specs=pl.BlockSpec((None, s, dk), lambda i, h: (i, 0, h)),
        out_shape=jax.ShapeDtypeStruct((b, s, n), bf16),
        scratch_shapes=[pltpu.VMEM((s, dk), f32), pltpu.VMEM((HG_VAL_DIM, dk), f32)],
        compiler_params=_cp("parallel", "parallel"), name="hgrn",
    )(p3, p3, p3, p3, p3, lb, norm_g.reshape(1, n))


def _mla_prep_kernel(cq_ref, ckv_ref, misc_ref, qg_ref, kvg_ref, wq_ref, wkv_ref, cos_ref, sin_ref,
                     q_out, k_out, v_out):
    h = MLA_HEADS
    scale = (MLA_NOPE + MLA_ROPE) ** -0.5
    cos, sin = cos_ref[...], sin_ref[...]
    qf = _dot(_rms(cq_ref[...].astype(f32), qg_ref[...]).astype(bf16), wq_ref[...]) * scale
    kv = _dot(_rms(ckv_ref[...].astype(f32), kvg_ref[...]).astype(bf16), wkv_ref[...])
    kr = _rope(misc_ref[...].astype(f32), cos, sin)[:, 0:MLA_ROPE].astype(bf16)
    ro = h * MLA_NOPE
    for pp in range(h * MLA_ROPE // LANE):
        qr = _rope(qf[:, ro + pp * LANE:ro + (pp + 1) * LANE], cos, sin).astype(bf16)
        for e in range(LANE // MLA_ROPE):
            q_out[pp * (LANE // MLA_ROPE) + e, :, MLA_NOPE:MLA_NOPE + MLA_ROPE] = qr[:, e * MLA_ROPE:(e + 1) * MLA_ROPE]
    for hh in range(h):
        q_out[hh, :, 0:MLA_NOPE] = qf[:, hh * MLA_NOPE:(hh + 1) * MLA_NOPE].astype(bf16)
        k_out[hh, :, 0:MLA_NOPE] = kv[:, hh * MLA_NOPE:(hh + 1) * MLA_NOPE].astype(bf16)
        k_out[hh, :, MLA_NOPE:MLA_NOPE + MLA_ROPE] = kr
        v_out[hh] = kv[:, ro + hh * MLA_V:ro + (hh + 1) * MLA_V].astype(bf16)


def _mla_prep(p, q_norm_g, kv_norm_g, w_uq_p, w_ukv_p, cos_s, sin_s, S):
    t = p.shape[0]
    tm = ROW_TILE
    ns = S // tm
    h, qd = MLA_HEADS, MLA_NOPE + MLA_ROPE
    return pl.pallas_call(
        _mla_prep_kernel, grid=(t // tm,),
        in_specs=[pl.BlockSpec((tm, MLA_Q_RANK), lambda i: (i, _DST['mla_cq'] // MLA_Q_RANK)),
                  pl.BlockSpec((tm, MLA_KV_RANK), lambda i: (i, _DST['mla_ckv'] // MLA_KV_RANK)),
                  pl.BlockSpec((tm, LANE), lambda i: (i, _MISC // LANE)),
                  pl.BlockSpec((1, MLA_Q_RANK), lambda i: (0, 0)),
                  pl.BlockSpec((1, MLA_KV_RANK), lambda i: (0, 0)),
                  pl.BlockSpec(w_uq_p.shape, lambda i: (0, 0)),
                  pl.BlockSpec(w_ukv_p.shape, lambda i: (0, 0)),
                  pl.BlockSpec((tm, LANE), lambda i: (i % ns, 0)),
                  pl.BlockSpec((tm, LANE), lambda i: (i % ns, 0))],
        out_specs=[pl.BlockSpec((h, tm, qd), lambda i: (0, i, 0)),
                   pl.BlockSpec((h, tm, qd), lambda i: (0, i, 0)),
                   pl.BlockSpec((h, tm, MLA_V), lambda i: (0, i, 0))],
        out_shape=[jax.ShapeDtypeStruct((h, t, qd), bf16), jax.ShapeDtypeStruct((h, t, qd), bf16),
                   jax.ShapeDtypeStruct((h, t, MLA_V), bf16)],
        compiler_params=_cp("parallel"), name="mla_prep",
    )(p, p, p, q_norm_g.reshape(1, -1), kv_norm_g.reshape(1, -1), w_uq_p, w_ukv_p, cos_s, sin_s)


def _mla_attn_kernel(q_ref, k_ref, v_ref, o_ref, *, LC, j0, tq):
    jb = pl.program_id(2) + j0
    s = _dot_nt(q_ref[...], k_ref[...])
    key = lax.broadcasted_iota(jnp.int32, (1, s.shape[1]), 1)
    s = jnp.where((jb < LC // tq) & (key >= LC), -1e30, s)
    e = jnp.exp(s - jnp.max(s, axis=1, keepdims=True))
    o = _dot(e.astype(bf16), v_ref[...]) / jnp.sum(e, axis=1, keepdims=True)
    o_ref[...] = o.astype(o_ref.dtype)


def _mla_attn(qh, kh, vh, B, LC, LL, lat_only):
    h, t, qd = qh.shape
    s = LC + LL
    tq = ROW_TILE
    j0 = LC // tq if lat_only else 0
    nq = s // tq - j0
    ns = s // tq
    return pl.pallas_call(
        functools.partial(_mla_attn_kernel, LC=LC, j0=j0, tq=tq), grid=(B, h, nq),
        in_specs=[pl.BlockSpec((None, tq, qd), lambda b, hh, j: (hh, b * ns + j + j0, 0)),
                  pl.BlockSpec((None, s, qd), lambda b, hh, j: (hh, b, 0)),
                  pl.BlockSpec((None, s, MLA_V), lambda b, hh, j: (hh, b, 0))],
        out_specs=pl.BlockSpec((tq, MLA_V), lambda b, hh, j: (b * ns + j + j0, hh)),
        out_shape=jax.ShapeDtypeStruct((t, h * MLA_V), bf16),
        compiler_params=_cp("parallel", "parallel", "parallel"), name="mla_attn",
    )(qh, kh, vh)


def _merge_kernel(y0, y1, y2, y3, g0, g1, g2, g3, w_ref, o_ref):
    acc = None
    for n, (y_ref, g_ref) in enumerate(((y0, g0), (y1, g1), (y2, g2), (y3, g3))):
        t = _sigmoid(g_ref[...].astype(f32)) * _dot(y_ref[...], w_ref[n])
        acc = t if acc is None else acc + t
    o_ref[...] = acc.astype(o_ref.dtype)


def _merge(ys, p, w_branch_b, rows):
    t = p.shape[0]
    d = D_MODEL
    tm, tn = rows.tm, 512
    gb = _DST['gates'] // tn
    y_specs = [pl.BlockSpec((tm, BRANCH_W), lambda i, j: (rows.src(i), 0)) for _ in range(N_BRANCH)]
    g_specs = [pl.BlockSpec((tm, tn), functools.partial(lambda i, j, n: (rows.src(i), gb + n * (d // tn) + j), n=n))
               for n in range(N_BRANCH)]
    return pl.pallas_call(
        _merge_kernel, grid=(rows.n, d // tn),
        in_specs=y_specs + g_specs + [pl.BlockSpec((N_BRANCH, BRANCH_W, tn), lambda i, j: (0, 0, j))],
        out_specs=pl.BlockSpec((tm, tn), lambda i, j: (i, j)),
        out_shape=jax.ShapeDtypeStruct((rows.n * tm, d), bf16),
        compiler_params=_cp("parallel", "parallel"), name="merge",
    )(*ys, p, p, p, p, w_branch_b)


def _out_proj_kernel(m_ref, w_ref, x_ref, mod_ref, gpost_ref, gffn_ref, rw_ref, x_out, h_out, l_out):
    r = _rms(_dot(m_ref[...], w_ref[...]), gpost_ref[...])
    x = x_ref[...] + mod_ref[2:3, :] * r
    x_out[...] = x
    h2 = _rms(x, gffn_ref[...]) * (1.0 + mod_ref[4:5, :]) + mod_ref[3:4, :]
    h_out[...] = h2.astype(h_out.dtype)
    l_out[...] = _dot_exact(h2, rw_ref[...])


def _out_proj(m, w_out_b, x, mods, g_post, g_ffn, router_w, rows):
    d = D_MODEL
    tm = rows.tm
    n = rows.n * tm
    ne = router_w.shape[1]
    return pl.pallas_call(
        _out_proj_kernel, grid=(rows.n,),
        in_specs=[pl.BlockSpec((tm, d), lambda i: (i, 0)),
                  pl.BlockSpec((d, d), lambda i: (0, 0)),
                  pl.BlockSpec((tm, d), lambda i: (rows.src(i), 0)),
                  pl.BlockSpec((None, 6, d), lambda i: (rows.cond(i), 0, 0)),
                  pl.BlockSpec((1, d), lambda i: (0, 0)),
                  pl.BlockSpec((1, d), lambda i: (0, 0)),
                  pl.BlockSpec((d, ne), lambda i: (0, 0))],
        out_specs=[pl.BlockSpec((tm, d), lambda i: (i, 0)),
                   pl.BlockSpec((tm, d), lambda i: (i, 0)),
                   pl.BlockSpec((tm, ne), lambda i: (i, 0))],
        out_shape=[jax.ShapeDtypeStruct((n, d), f32), jax.ShapeDtypeStruct((n, d), bf16),
                   jax.ShapeDtypeStruct((n, ne), f32)],
        compiler_params=_cp("parallel"), name="out_proj",
    )(m, w_out_b, x, mods, g_post.reshape(1, d), g_ffn.reshape(1, d), router_w)


def _expert_kernel(be_ref, nu_ref, x_ref, w_ref, wg_ref, wu_ref, wd_ref, o_ref, wg_b, wu_b, wd_b):
    i = pl.program_id(0)

    @pl.when((i == 0) | (be_ref[i] != be_ref[jnp.maximum(i - 1, 0)]))
    def _():
        wg_b[...] = wg_ref[...].astype(bf16)
        wu_b[...] = wu_ref[...].astype(bf16)
        wd_b[...] = wd_ref[...].astype(bf16)

    @pl.when(i < nu_ref[0])
    def _():
        x = x_ref[...]
        hmid = _silu(_dot(x, wg_b[...])) * _dot(x, wu_b[...])
        o_ref[...] = (_dot(hmid.astype(bf16), wd_b[...]) * w_ref[...]).astype(o_ref.dtype)

    @pl.when(i >= nu_ref[0])
    def _():
        o_ref[...] = jnp.zeros_like(o_ref)


def _experts(x_sorted, row_w, blk_e, n_used, wg, wu, wd, layer):
    n_rows, d = x_sorted.shape
    ff = wg.shape[3]
    bm = MOE_ROWS
    grid_spec = pltpu.PrefetchScalarGridSpec(
        num_scalar_prefetch=2, grid=(n_rows // bm,),
        in_specs=[pl.BlockSpec((bm, d), lambda i, be, nu: (i, 0)),
                  pl.BlockSpec((bm, 1), lambda i, be, nu: (i, 0)),
                  pl.BlockSpec((None, None, d, ff), lambda i, be, nu: (layer, be[i], 0, 0)),
                  pl.BlockSpec((None, None, d, ff), lambda i, be, nu: (layer, be[i], 0, 0)),
                  pl.BlockSpec((None, None, ff, d), lambda i, be, nu: (layer, be[i], 0, 0))],
        out_specs=pl.BlockSpec((bm, d), lambda i, be, nu: (i, 0)),
        scratch_shapes=[pltpu.VMEM((d, ff), bf16), pltpu.VMEM((d, ff), bf16), pltpu.VMEM((ff, d), bf16)])
    return pl.pallas_call(
        _expert_kernel, grid_spec=grid_spec,
        out_shape=jax.ShapeDtypeStruct((n_rows, d), bf16),
        compiler_params=_cp("arbitrary"), name="experts",
    )(blk_e, n_used, x_sorted, row_w, wg, wu, wd)


def _ffn_out_kernel(h_ref, r_ref, x_ref, mod_ref, g_ref, sg_ref, su_ref, sd_ref, o_ref):
    h = h_ref[...]
    mid = _silu(_dot(h, sg_ref[...])) * _dot(h, su_ref[...])
    f = _dot(mid.astype(bf16), sd_ref[...]) + r_ref[...]
    o_ref[...] = x_ref[...] + mod_ref[5:6, :] * _rms(f, g_ref[...])


def _ffn_out(h2, routed, x, mods, g_post, sg, su, sd, rows):
    n, d = h2.shape
    tm = rows.tm
    ff = sg.shape[1]
    return pl.pallas_call(
        _ffn_out_kernel, grid=(rows.n,),
        in_specs=[pl.BlockSpec((tm, d), lambda i: (i, 0)),
                  pl.BlockSpec((tm, d), lambda i: (i, 0)),
                  pl.BlockSpec((tm, d), lambda i: (i, 0)),
                  pl.BlockSpec((None, 6, d), lambda i: (rows.cond(i), 0, 0)),
                  pl.BlockSpec((1, d), lambda i: (0, 0)),
                  pl.BlockSpec((d, ff), lambda i: (0, 0)),
                  pl.BlockSpec((d, ff), lambda i: (0, 0)),
                  pl.BlockSpec((ff, d), lambda i: (0, 0))],
        out_specs=pl.BlockSpec((tm, d), lambda i: (i, 0)),
        out_shape=jax.ShapeDtypeStruct((n, d), f32),
        compiler_params=_cp("parallel"), name="ffn_out",
    )(h2, routed, x, mods, g_post.reshape(1, d), sg, su, sd)


def _route_kernel(l_ref, b_ref, e_out, w_out):
    scores = _sigmoid(l_ref[...])
    sel = scores + b_ref[...]
    tm, ne = sel.shape
    per_group = ne // N_EXPERT_GROUPS
    lane = lax.broadcasted_iota(jnp.int32, (tm, ne), 1)
    lane_f = lane.astype(f32)
    grp = lane // per_group
    neg = -jnp.inf

    def first_max(v):
        m = jnp.max(v, axis=1, keepdims=True)
        return m, jnp.min(jnp.where(v == m, lane_f, float(ne)), axis=1, keepdims=True)

    gs = []
    for g in range(N_EXPERT_GROUPS):
        sg = jnp.where(grp == g, sel, neg)
        m1, i1 = first_max(sg)
        gs.append(m1 + jnp.max(jnp.where(lane_f == i1, neg, sg), axis=1, keepdims=True))
    chosen = [jnp.zeros((tm, 1), jnp.bool_)] * N_EXPERT_GROUPS
    for _ in range(TOPK_EXPERT_GROUPS):
        m = functools.reduce(jnp.maximum, gs)
        found = jnp.zeros((tm, 1), jnp.bool_)
        for g in range(N_EXPERT_GROUPS):
            hit = (gs[g] == m) & ~found
            chosen[g] = chosen[g] | hit
            found = found | hit
            gs[g] = jnp.where(hit, neg, gs[g])
    keep = jnp.zeros((tm, ne), jnp.bool_)
    for g in range(N_EXPERT_GROUPS):
        keep = keep | ((grp == g) & chosen[g])
    cur = jnp.where(keep, sel, neg)
    ids, wts = [], []
    for _ in range(TOP_K):
        _, idx = first_max(cur)
        hit = lane_f == idx
        ids.append(idx)
        wts.append(jnp.sum(jnp.where(hit, scores, 0.0), axis=1, keepdims=True))
        cur = jnp.where(hit, neg, cur)
    norm = ROUTE_SCALE / functools.reduce(lambda a, b: a + b, wts)
    out_lane = lax.broadcasted_iota(jnp.int32, (tm, LANE), 1)
    e = jnp.zeros((tm, LANE), f32)
    w = jnp.zeros((tm, LANE), f32)
    for k in range(TOP_K):
        e = jnp.where(out_lane == k, ids[k], e)
        w = jnp.where(out_lane == k, wts[k] * norm, w)
    e_out[...] = e.astype(jnp.int32)
    w_out[...] = w


def _route(logits, router_bias):
    t, ne = logits.shape
    tm = ROW_TILE
    e, w = pl.pallas_call(
        _route_kernel, grid=(t // tm,),
        in_specs=[pl.BlockSpec((tm, ne), lambda i: (i, 0)), pl.BlockSpec((1, ne), lambda i: (0, 0))],
        out_specs=[pl.BlockSpec((tm, LANE), lambda i: (i, 0)), pl.BlockSpec((tm, LANE), lambda i: (i, 0))],
        out_shape=[jax.ShapeDtypeStruct((t, LANE), jnp.int32), jax.ShapeDtypeStruct((t, LANE), f32)],
        compiler_params=_cp("parallel"), name="route",
    )(logits, router_bias.astype(f32).reshape(1, ne))
    return e[:, :TOP_K], w[:, :TOP_K]


def _moe(h2, logits, router_bias, wg, wu, wd, layer):
    t, d = h2.shape
    top_e, top_w = _route(logits, router_bias)
    n_assign = t * TOP_K
    flat_e = top_e.reshape(n_assign)
    onehot = (flat_e[:, None] == jnp.arange(N_EXPERTS)[None, :]).astype(jnp.int32)
    csum = jnp.cumsum(onehot, axis=0)
    counts = csum[-1]
    rank = jnp.take_along_axis(csum, flat_e[:, None], axis=1)[:, 0] - 1
    padded = (counts + MOE_ROWS - 1) // MOE_ROWS * MOE_ROWS
    pad_end = jnp.cumsum(padded)
    dest = (pad_end - padded)[flat_e] + rank
    n_blocks = -(-n_assign // MOE_ROWS) + N_EXPERTS
    n_rows = n_blocks * MOE_ROWS
    row_tok = jnp.zeros((n_rows,), jnp.int32).at[dest].set(jnp.arange(n_assign, dtype=jnp.int32) // TOP_K)
    row_w = jnp.zeros((n_rows,), f32).at[dest].set(top_w.reshape(n_assign))
    blk_e = jnp.minimum(jnp.searchsorted(pad_end, jnp.arange(n_blocks) * MOE_ROWS, side='right'),
                        N_EXPERTS - 1).astype(jnp.int32)
    n_used = (pad_end[-1] // MOE_ROWS).astype(jnp.int32).reshape(1)
    x_sorted = jnp.take(h2, row_tok, axis=0)
    y_sorted = _experts(x_sorted, row_w.reshape(n_rows, 1), blk_e, n_used, wg, wu, wd, layer)
    dest_k = dest.reshape(t, TOP_K).T.reshape(n_assign)
    return jnp.take(y_sorted, dest_k, axis=0).reshape(TOP_K, t, d).astype(f32).sum(axis=0)


def _permute_w_in(w_in):
    cols = [w_in[:, _SRC[name][0]:_SRC[name][0] + _SRC[name][1]] for name in _P_ORDER]
    cols.append(jnp.zeros((w_in.shape[0], _P_WIDTH - _P_USED), w_in.dtype))
    return jnp.concatenate(cols, axis=1).astype(bf16)


def _permute_mla(w_uq, w_ukv):
    h = MLA_HEADS
    q = w_uq.reshape(-1, h, MLA_NOPE + MLA_ROPE)
    w_uq_p = jnp.concatenate([q[:, :, :MLA_NOPE].reshape(-1, h * MLA_NOPE), q[:, :, MLA_NOPE:].reshape(-1, h * MLA_ROPE)], 1)
    kv = w_ukv.reshape(-1, h, MLA_NOPE + MLA_V)
    w_ukv_p = jnp.concatenate([kv[:, :, :MLA_NOPE].reshape(-1, h * MLA_NOPE), kv[:, :, MLA_NOPE:].reshape(-1, h * MLA_V)], 1)
    return w_uq_p.astype(bf16), w_ukv_p.astype(bf16)


def _rope_tables(LC, LL):
    pos = jnp.arange(LL)
    n_freq = SWA_HEAD_DIM // 4
    inv_freq = ROPE_BASE ** (-jnp.arange(n_freq, dtype=f32) / n_freq)
    ang = jnp.concatenate([(pos // GRID_W).astype(f32)[:, None] * inv_freq,
                           (pos % GRID_W).astype(f32)[:, None] * inv_freq], axis=-1)
    cos, sin = jnp.cos(ang), jnp.sin(ang)
    cos = jnp.concatenate([jnp.ones((LC, cos.shape[1]), f32), cos], axis=0)
    sin = jnp.concatenate([jnp.zeros((LC, sin.shape[1]), f32), sin], axis=0)
    return jnp.tile(jnp.concatenate([cos, cos], 1), (1, 2)), jnp.tile(jnp.concatenate([-sin, sin], 1), (1, 2))


def _dir_lanes(v):
    out = jnp.zeros((2, LANE), f32)
    for d in range(2):
        out = out.at[d, _DT_LANE + d * SSD_HEADS:_DT_LANE + (d + 1) * SSD_HEADS].set(v[d].astype(f32))
    return out


def kernel(x, c, ctx, c_ctx, w_mod, b_mod, norm_pre_mix, norm_post_mix, norm_pre_ffn, norm_post_ffn, w_in, ssd_conv_w, ssd_conv_b, ssd_dt_bias, ssd_a_log, ssd_d, ssd_norm, swa_sink, hg_lb_logits, hg_norm, mla_q_norm, mla_kv_norm, mla_w_uq, mla_w_ukv, w_branch, w_out, router_w, router_bias, expert_w_gate, expert_w_up, expert_w_down, shared_w_gate, shared_w_up, shared_w_down):
    B, LL, d = x.shape
    LC = ctx.shape[1]
    S = LC + LL
    depth = w_mod.shape[0]
    assert LC % ROW_TILE == 0 and LL % ROW_TILE == 0 and LL >= 3 * WINDOW and d == D_MODEL

    n_cond = -(-(B + 1) // 8) * 8
    cond = jnp.zeros((n_cond, d), f32).at[:B].set(c).at[B].set(c_ctx)
    mods_all = _modulation(cond, w_mod, b_mod).reshape(depth, n_cond, 6, d)
    cos_s, sin_s = _rope_tables(LC, LL)
    sm = jax.nn.softmax(hg_lb_logits.astype(f32), axis=0)
    lower_bounds = jnp.cumsum(sm, axis=0) - sm[0]

    xs = jnp.concatenate([ctx, x], axis=1).reshape(B * S, d)
    for l in range(depth):
        last = l == depth - 1
        mods = mods_all[l]
        all_rows = _Rows(B, LC, LL, False)
        out_rows = _Rows(B, LC, LL, last)
        h = _norm_mod(xs, norm_pre_mix[l], mods, all_rows)
        p = _matmul(h, _permute_w_in(w_in[l]), 1024, _P_TN, bf16, "in_proj")
        p3 = p.reshape(B, S, _P_WIDTH)

        xbc3 = _ssd_conv(p3, ssd_conv_w[l], ssd_conv_b[l], LC, LL)
        y_ssd = _ssd_scan(xbc3, p3, _dir_lanes(ssd_dt_bias[l]), _dir_lanes(-jnp.exp(ssd_a_log[l].astype(f32))), LC, LL)
        d_in = SSD_HEADS * SSD_HEAD_DIM
        y_ssd = _ssd_finish(y_ssd.reshape(B * S, d_in), xbc3.reshape(B * S, -1), p,
                            jnp.repeat(ssd_d[l].astype(f32), SSD_HEAD_DIM).reshape(1, d_in), ssd_norm[l])
        y_swa = _swa(p3, swa_sink[l].astype(f32), cos_s, sin_s, LC, LL, last).reshape(B * S, -1)
        y_hg = _hgrn(p3, lower_bounds[l], hg_norm[l], LC, LL).reshape(B * S, -1)
        w_uq_p, w_ukv_p = _permute_mla(mla_w_uq[l], mla_w_ukv[l])
        qh, kh, vh = _mla_prep(p, mla_q_norm[l], mla_kv_norm[l], w_uq_p, w_ukv_p, cos_s, sin_s, S)
        y_mla = _mla_attn(qh, kh, vh, B, LC, LL, last)

        m = _merge((y_ssd, y_swa, y_hg, y_mla), p, w_branch[l].astype(bf16), out_rows)
        x_mid, h2, logits = _out_proj(m, w_out[l].astype(bf16), xs, mods, norm_post_mix[l], norm_pre_ffn[l],
                                      router_w[l], out_rows)
        routed = _moe(h2, logits, router_bias[l], expert_w_gate, expert_w_up, expert_w_down, l)
        xs = _ffn_out(h2, routed, x_mid, mods, norm_post_ffn[l], shared_w_gate[l].astype(bf16),
                      shared_w_up[l].astype(bf16), shared_w_down[l].astype(bf16), out_rows)
    return xs.reshape(B, LL, d)
```

```python
import functools
import math

import jax
import jax.numpy as jnp
import numpy as np
from jax import lax
from jax.experimental import pallas as pl
from jax.experimental.pallas import tpu as pltpu

f32 = jnp.float32
bf16 = jnp.bfloat16

D_MODEL = 2048
GRID_W = 64
N_BRANCH = 4
BRANCH_W = 1024
EPS = 1e-6
ROPE_BASE = 10000.0

SSD_HEADS = 16
SSD_HEAD_DIM = 64
SSD_GROUPS = 4
SSD_STATE = 128
SSD_CONV = 5
SSD_CHUNK = 128

SWA_HEADS = 16
SWA_KV_HEADS = 4
SWA_HEAD_DIM = 64
WINDOW = 128

HG_HEADS = 8
HG_KEY_DIM = 128
HG_VAL_DIM = 128
HG_CHUNK = 64
HG_DIAG = 8

MLA_HEADS = 8
MLA_Q_RANK = 512
MLA_KV_RANK = 256
MLA_NOPE = 128
MLA_ROPE = 64
MLA_V = 128

N_EXPERTS = 64
TOP_K = 6
N_EXPERT_GROUPS = 8
TOPK_EXPERT_GROUPS = 4
EXPERT_FF = 512
SHARED_FF = 1024
ROUTE_SCALE = 2.5
MOE_ROWS = 256

LANE = 128
ROW_TILE = 256
VMEM_LIMIT = 48 * 1024 * 1024

_IN_LAYOUT = (
    ('ssd_z', 1024), ('ssd_x', 1024), ('ssd_B', 512), ('ssd_C', 512), ('ssd_dt', 32),
    ('swa_q', 1024), ('swa_k', 256), ('swa_v', 256),
    ('hg_q', 1024), ('hg_f', 2048), ('hg_i', 1024), ('hg_g', 1024),
    ('mla_cq', 512), ('mla_ckv', 256), ('mla_kr', 64), ('gates', 8192),
)
_P_ORDER = ('gates', 'hg_f', 'ssd_x', 'ssd_B', 'ssd_C', 'ssd_z', 'swa_q', 'hg_q', 'hg_i', 'hg_g',
            'mla_cq', 'swa_k', 'swa_v', 'mla_ckv', 'mla_kr', 'ssd_dt')
_P_TN = 512


def _layout():
    src, off = {}, 0
    for name, size in _IN_LAYOUT:
        src[name] = (off, size)
        off += size
    dst, off = {}, 0
    for name in _P_ORDER:
        dst[name] = off
        off += src[name][1]
    width = -(-off // _P_TN) * _P_TN
    return src, dst, off, width


_SRC, _DST, _P_USED, _P_WIDTH = _layout()
_MISC = _DST['mla_kr']
_DT_LANE = _DST['ssd_dt'] - _MISC


def _cp(*sem):
    return pltpu.CompilerParams(dimension_semantics=sem, vmem_limit_bytes=VMEM_LIMIT)


def _sigmoid(x):
    return 1.0 / (1.0 + jnp.exp(-x))


def _silu(x):
    return x * _sigmoid(x)


def _rms(x, gain):
    return x * lax.rsqrt(jnp.mean(x * x, axis=-1, keepdims=True) + EPS) * gain


def _dot(a, b):
    return jnp.dot(a, b, preferred_element_type=f32)


def _dot_nt(a, b):
    return lax.dot_general(a, b, (((1,), (1,)), ((), ())), preferred_element_type=f32)


def _dot_tn(a, b):
    return lax.dot_general(a, b, (((0,), (0,)), ((), ())), preferred_element_type=f32)


def _dot_exact(a, b):
    return jnp.dot(a, b, preferred_element_type=f32, precision=lax.Precision.HIGHEST)


class _Rows:
    def __init__(self, B, LC, LL, lat_only, tm=ROW_TILE):
        self.B, self.LC, self.LL, self.tm = B, LC, LL, tm
        self.nC, self.nL = LC // tm, LL // tm
        self.nS = self.nC + self.nL
        self.lat_only = lat_only
        self.n = B * (self.nL if lat_only else self.nS)

    def src(self, i):
        if self.lat_only:
            return (i // self.nL) * self.nS + self.nC + i % self.nL
        return i

    def cond(self, i):
        if self.lat_only:
            return i // self.nL
        return jnp.where(i % self.nS < self.nC, self.B, i // self.nS)

    def pos(self, i):
        if self.lat_only:
            return self.nC + i % self.nL
        return i % self.nS


def _mod_kernel(c_ref, w_ref, b_ref, o_ref):
    s = _silu(c_ref[...]).astype(bf16)
    o_ref[...] = _dot(s, w_ref[...].astype(bf16)) + b_ref[...]


def _modulation(cond, w_mod, b_mod):
    nl, d, n = w_mod.shape
    nc = cond.shape[0]
    tn = 1024
    return pl.pallas_call(
        _mod_kernel, grid=(nl, n // tn),
        in_specs=[pl.BlockSpec((nc, d), lambda l, j: (0, 0)),
                  pl.BlockSpec((None, d, tn), lambda l, j: (l, 0, j)),
                  pl.BlockSpec((None, 1, tn), lambda l, j: (l, 0, j))],
        out_specs=pl.BlockSpec((None, nc, tn), lambda l, j: (l, 0, j)),
        out_shape=jax.ShapeDtypeStruct((nl, nc, n), f32),
        compiler_params=_cp("parallel", "parallel"), name="modulation",
    )(cond, w_mod, b_mod.reshape(nl, 1, n))


def _norm_mod_kernel(x_ref, g_ref, mod_ref, o_ref):
    y = _rms(x_ref[...], g_ref[...])
    o_ref[...] = (y * (1.0 + mod_ref[1:2, :]) + mod_ref[0:1, :]).astype(o_ref.dtype)


def _norm_mod(x, gain, mods, rows):
    t, d = x.shape
    tm = rows.tm
    return pl.pallas_call(
        _norm_mod_kernel, grid=(rows.n,),
        in_specs=[pl.BlockSpec((tm, d), lambda i: (rows.src(i), 0)),
                  pl.BlockSpec((1, d), lambda i: (0, 0)),
                  pl.BlockSpec((None, 6, d), lambda i: (rows.cond(i), 0, 0))],
        out_specs=pl.BlockSpec((tm, d), lambda i: (rows.src(i), 0)),
        out_shape=jax.ShapeDtypeStruct((t, d), bf16),
        compiler_params=_cp("parallel"), name="norm_mod",
    )(x, gain.reshape(1, d), mods)


def _mm_kernel(a_ref, b_ref, o_ref):
    o_ref[...] = _dot(a_ref[...], b_ref[...]).astype(o_ref.dtype)


def _matmul(a, b, tm, tn, out_dtype, name):
    m, k = a.shape
    n = b.shape[1]
    return pl.pallas_call(
        _mm_kernel, grid=(m // tm, n // tn),
        in_specs=[pl.BlockSpec((tm, k), lambda i, j: (i, 0)),
                  pl.BlockSpec((k, tn), lambda i, j: (0, j))],
        out_specs=pl.BlockSpec((tm, tn), lambda i, j: (i, j)),
        out_shape=jax.ShapeDtypeStruct((m, n), out_dtype),
        compiler_params=_cp("parallel", "parallel"), name=name,
    )(a, b)


def _ssd_conv_kernel(x_ref, w_ref, b_ref, o_ref, *, LC, LL):
    s = LC + LL
    x = x_ref[...].astype(f32)
    row = lax.broadcasted_iota(jnp.int32, (s, 1), 0)
    pos = jnp.where(row < LC, row, row - LC)
    length = jnp.where(row < LC, LC, LL)
    acc = jnp.zeros_like(x) + b_ref[...]
    half = SSD_CONV // 2
    for k in range(SSD_CONV):
        sh = k - half
        xs = x if sh == 0 else pltpu.roll(x, (-sh) % s, 0)
        valid = (pos + sh >= 0) & (pos + sh < length)
        acc = acc + w_ref[k:k + 1, :] * jnp.where(valid, xs, 0.0)
    o_ref[...] = _silu(acc).astype(o_ref.dtype)


def _ssd_conv(p3, conv_w, conv_b, LC, LL):
    b, s, _ = p3.shape
    n = conv_w.shape[1]
    tc = 256
    base = _DST['ssd_x'] // tc
    return pl.pallas_call(
        functools.partial(_ssd_conv_kernel, LC=LC, LL=LL), grid=(b, n // tc),
        in_specs=[pl.BlockSpec((None, s, tc), lambda i, j: (i, 0, base + j)),
                  pl.BlockSpec((SSD_CONV, tc), lambda i, j: (0, j)),
                  pl.BlockSpec((1, tc), lambda i, j: (0, j))],
        out_specs=pl.BlockSpec((None, s, tc), lambda i, j: (i, 0, j)),
        out_shape=jax.ShapeDtypeStruct((b, s, n), bf16),
        compiler_params=_cp("parallel", "parallel"), name="ssd_conv",
    )(p3, conv_w, conv_b.reshape(1, n))


def _chunk_start(c, n_ctx, n_all, q, rev):
    if rev:
        c = jnp.where(c < n_ctx, n_ctx - 1 - c, n_all - 1 - (c - n_ctx))
    return pl.multiple_of(c * q, q)


def _pick4(hid, vals):
    return jnp.where(hid == 0, vals[0], jnp.where(hid == 1, vals[1], jnp.where(hid == 2, vals[2], vals[3])))


def _ssd_scan_kernel(x_ref, b_ref, c_ref, dt_ref, bias_ref, a_ref, o_ref, st_ref, *, LC, LL):
    q = SSD_CHUNK
    r = SSD_HEADS // SSD_GROUPS
    g = pl.program_id(1)
    n_ctx, n_all = LC // q, (LC + LL) // q
    ii = lax.broadcasted_iota(jnp.int32, (q, q), 0)
    jj = lax.broadcasted_iota(jnp.int32, (q, q), 1)
    lane = lax.broadcasted_iota(jnp.int32, (1, LANE), 1)
    sub = lax.broadcasted_iota(jnp.int32, (LANE, 1), 0)
    hid = lax.broadcasted_iota(jnp.int32, (1, r * SSD_HEAD_DIM), 1) // SSD_HEAD_DIM
    lane_lo = lane < SSD_HEAD_DIM

    for d in range(2):
        rev = d == 1
        tri = (ii <= jj) if rev else (ii >= jj)
        tri_f = tri.astype(f32)
        last = 0 if rev else q - 1
        st_ref[...] = jnp.zeros_like(st_ref)

        def chunk(c, carry, d=d, rev=rev, tri=tri, tri_f=tri_f, last=last):
            s = _chunk_start(c, n_ctx, n_all, q, rev)
            raw = dt_ref[pl.ds(s, q), :].astype(f32) + bias_ref[d:d + 1, :]
            dtv = jnp.maximum(raw, 0.0) + jnp.log1p(jnp.exp(-jnp.abs(raw)))
            a = dtv * a_ref[d:d + 1, :]
            acum = _dot_exact(tri_f, a)
            acum_t = acum.T
            x = x_ref[pl.ds(s, q), :].astype(f32)
            bm = b_ref[pl.ds(s, q), :]
            cm = c_ref[pl.ds(s, q), :]
            cb = _dot_nt(cm, bm)
            acol, arow, dcol, alast = [], [], [], []
            for hh in range(r):
                col = _DT_LANE + d * SSD_HEADS + g * r + hh
                acol.append(jnp.sum(jnp.where(lane == col, acum, 0.0), axis=1, keepdims=True))
                arow.append(jnp.sum(jnp.where(sub == col, acum_t, 0.0), axis=0, keepdims=True))
                dcol.append(jnp.sum(jnp.where(lane == col, dtv, 0.0), axis=1, keepdims=True))
                alast.append(acol[hh][last:last + 1, :])
            a_all = _pick4(hid, acol)
            al_all = _pick4(hid, alast)
            dtx = _pick4(hid, dcol) * x
            st = st_ref[...]
            y_off = _dot(cm, st.astype(bf16)) * jnp.exp(a_all)
            y_diag = []
            for pp in range(r // 2):
                decay = [jnp.exp(jnp.where(tri, acol[h] - arow[h], -1e30)) * cb for h in (2 * pp, 2 * pp + 1)]
                lhs = jnp.concatenate(decay, axis=1).astype(bf16)
                xp = dtx[:, pp * LANE:(pp + 1) * LANE]
                rhs = jnp.concatenate([jnp.where(lane_lo, xp, 0.0), jnp.where(lane_lo, 0.0, xp)], axis=0)
                y_diag.append(_dot(lhs, rhs.astype(bf16)))
            y = y_off + jnp.concatenate(y_diag, axis=1)
            if rev:
                o_ref[pl.ds(s, q), :] += y
            else:
                o_ref[pl.ds(s, q), :] = y
            w = (jnp.exp(al_all - a_all) * dtx).astype(bf16)
            st_ref[...] = st * jnp.exp(al_all) + _dot_tn(bm, w)
            return carry

        lax.fori_loop(0, n_all, chunk, 0, unroll=2)


def _ssd_scan(xbc3, p3, bias_v, a_v, LC, LL):
    b, s, _ = xbc3.shape
    d_in = SSD_HEADS * SSD_HEAD_DIM
    gw = d_in // SSD_GROUPS
    nb = d_in // SSD_STATE
    return pl.pallas_call(
        functools.partial(_ssd_scan_kernel, LC=LC, LL=LL), grid=(b, SSD_GROUPS),
        in_specs=[pl.BlockSpec((None, s, gw), lambda i, g: (i, 0, g)),
                  pl.BlockSpec((None, s, SSD_STATE), lambda i, g: (i, 0, nb + g)),
                  pl.BlockSpec((None, s, SSD_STATE), lambda i, g: (i, 0, nb + SSD_GROUPS + g)),
                  pl.BlockSpec((None, s, LANE), lambda i, g: (i, 0, _MISC // LANE)),
                  pl.BlockSpec((2, LANE), lambda i, g: (0, 0)),
                  pl.BlockSpec((2, LANE), lambda i, g: (0, 0))],
        out_specs=pl.BlockSpec((None, s, gw), lambda i, g: (i, 0, g)),
        out_shape=jax.ShapeDtypeStruct((b, s, d_in), f32),
        scratch_shapes=[pltpu.VMEM((SSD_STATE, gw), f32)],
        compiler_params=_cp("parallel", "parallel"), name="ssd_scan",
    )(xbc3, xbc3, xbc3, p3, bias_v, a_v)


def _ssd_finish_kernel(y_ref, x_ref, z_ref, dsk_ref, g_ref, o_ref):
    y = y_ref[...] + x_ref[...].astype(f32) * dsk_ref[...]
    o_ref[...] = _rms(y * _silu(z_ref[...].astype(f32)), g_ref[...]).astype(o_ref.dtype)


def _ssd_finish(y, xbc, p, d_skip_row, norm_g):
    t, n = y.shape
    tm = ROW_TILE
    zb = _DST['ssd_z'] // n
    return pl.pallas_call(
        _ssd_finish_kernel, grid=(t // tm,),
        in_specs=[pl.BlockSpec((tm, n), lambda i: (i, 0)),
                  pl.BlockSpec((tm, n), lambda i: (i, 0)),
                  pl.BlockSpec((tm, n), lambda i: (i, zb)),
                  pl.BlockSpec((1, n), lambda i: (0, 0)),
                  pl.BlockSpec((1, n), lambda i: (0, 0))],
        out_specs=pl.BlockSpec((tm, n), lambda i: (i, 0)),
        out_shape=jax.ShapeDtypeStruct((t, n), bf16),
        compiler_params=_cp("parallel"), name="ssd_finish",
    )(y, xbc, p, d_skip_row, norm_g.reshape(1, n))


def _rope(x, cos, sin):
    lane = lax.broadcasted_iota(jnp.int32, (1, LANE), 1)
    first = (lane % 64) < 32
    partner = jnp.where(first, pltpu.roll(x, LANE - 32, 1), pltpu.roll(x, 32, 1))
    return x * cos + partner * sin


def _swa_kernel(sink_ref, q_ref, k_ref, v_ref, cos_ref, sin_ref, o_ref, *, LC, LL, j0):
    w = WINDOW
    hd = SWA_HEAD_DIM
    grp = SWA_HEADS // SWA_KV_HEADS
    kv_w = SWA_KV_HEADS * hd
    jb = pl.program_id(1) + j0
    n_ctx = LC // w
    is_lat = jb >= n_ctx
    nb = jnp.maximum(jb - n_ctx, 0)
    qs = pl.multiple_of(jb * w, w)
    k0 = pl.multiple_of(jnp.clip((nb - 1) * w, 0, LL - 3 * w), w)
    ks = pl.multiple_of(LC + k0, w)
    scale = hd ** -0.5

    cos_q, sin_q = cos_ref[pl.ds(qs, w), :], sin_ref[pl.ds(qs, w), :]
    cos_k, sin_k = cos_ref[pl.ds(ks, 3 * w), :], sin_ref[pl.ds(ks, 3 * w), :]
    q = q_ref[...].astype(f32)
    q = jnp.concatenate([_rope(q[:, c * LANE:(c + 1) * LANE], cos_q, sin_q) for c in range(SWA_HEADS * hd // LANE)],
                        axis=1) * scale
    kb = k_ref[pl.ds(ks, 3 * w), :].astype(f32)
    kb = jnp.concatenate([_rope(kb[:, c * LANE:(c + 1) * LANE], cos_k, sin_k) for c in range(kv_w // LANE)],
                         axis=1).astype(bf16)
    vb = v_ref[pl.ds(ks, 3 * w), :]
    kc = k_ref[0:LC, :]
    vc = v_ref[0:LC, :]

    rows = grp * w
    q_abs = nb * w + lax.broadcasted_iota(jnp.int32, (rows, 1), 0) % w
    k_abs = k0 + lax.broadcasted_iota(jnp.int32, (1, 3 * w), 1)
    valid = is_lat & (jnp.abs(k_abs - q_abs) <= w)
    head_of_row = lax.broadcasted_iota(jnp.int32, (rows, 1), 0) // w

    outs = []
    for g in range(SWA_KV_HEADS):
        qg = jnp.concatenate([q[:, (g * grp + r) * hd:(g * grp + r + 1) * hd] for r in range(grp)], axis=0).astype(bf16)
        sl = slice(g * hd, (g + 1) * hd)
        s_lat = jnp.where(valid, _dot_nt(qg, kb[:, sl]), -1e30)
        s_ctx = _dot_nt(qg, kc[:, sl])
        sink = jnp.zeros((rows, 1), f32)
        for r in range(grp):
            sink = jnp.where(head_of_row == r, sink_ref[g * grp + r], sink)
        m = jnp.maximum(jnp.maximum(jnp.max(s_lat, axis=1, keepdims=True), jnp.max(s_ctx, axis=1, keepdims=True)), sink)
        e_lat = jnp.exp(s_lat - m)
        e_ctx = jnp.exp(s_ctx - m)
        den = jnp.sum(e_lat, axis=1, keepdims=True) + jnp.sum(e_ctx, axis=1, keepdims=True) + jnp.exp(sink - m)
        og = (_dot(e_lat.astype(bf16), vb[:, sl]) + _dot(e_ctx.astype(bf16), vc[:, sl])) / den
        outs.extend(og[r * w:(r + 1) * w, :] for r in range(grp))
    o_ref[...] = jnp.concatenate(outs, axis=1).astype(o_ref.dtype)


def _swa(p3, sink, cos_s, sin_s, LC, LL, lat_only):
    b, s, _ = p3.shape
    w = WINDOW
    j0 = LC // w if lat_only else 0
    nq = s // w - j0
    qw = SWA_HEADS * SWA_HEAD_DIM
    kw = SWA_KV_HEADS * SWA_HEAD_DIM
    return pl.pallas_call(
        functools.partial(_swa_kernel, LC=LC, LL=LL, j0=j0), grid=(b, nq),
        in_specs=[pl.BlockSpec(memory_space=pltpu.SMEM),
                  pl.BlockSpec((None, w, qw), lambda i, j: (i, j + j0, _DST['swa_q'] // qw)),
                  pl.BlockSpec((None, s, kw), lambda i, j: (i, 0, _DST['swa_k'] // kw)),
                  pl.BlockSpec((None, s, kw), lambda i, j: (i, 0, _DST['swa_v'] // kw)),
                  pl.BlockSpec((s, LANE), lambda i, j: (0, 0)),
                  pl.BlockSpec((s, LANE), lambda i, j: (0, 0))],
        out_specs=pl.BlockSpec((None, w, qw), lambda i, j: (i, j + j0, 0)),
        out_shape=jax.ShapeDtypeStruct((b, s, qw), bf16),
        compiler_params=_cp("parallel", "parallel"), name="swa",
    )(sink, p3, p3, p3, cos_s, sin_s)


def _hgrn_kernel(q_ref, ff_ref, fr_ref, v_ref, g_ref, lb_ref, ng_ref, o_ref, acc_ref, st_ref, *, LC, LL):
    q = HG_CHUNK
    n_ctx, n_all = LC // q, (LC + LL) // q
    ii = lax.broadcasted_iota(jnp.int32, (q, q), 0)
    jj = lax.broadcasted_iota(jnp.int32, (q, q), 1)
    r8 = lax.broadcasted_iota(jnp.int32, (q, 1), 0) % HG_DIAG
    levels = []
    sz = HG_DIAG
    while sz < q:
        levels.append(sz)
        sz *= 2

    for d in range(2):
        rev = d == 1
        tri_f = ((ii <= jj) if rev else (ii >= jj)).astype(f32)
        last = 0 if rev else q - 1
        f_ref = fr_ref if rev else ff_ref
        st_ref[...] = jnp.zeros_like(st_ref)
        masks = []
        for sz in levels:
            same = (ii // (2 * sz)) == (jj // (2 * sz))
            i_hi, j_hi = (ii % (2 * sz)) >= sz, (jj % (2 * sz)) >= sz
            masks.append(same & (~i_hi & j_hi if rev else i_hi & ~j_hi))

        def chunk(c, carry, d=d, rev=rev, tri_f=tri_f, last=last, f_ref=f_ref, masks=masks):
            s = _chunk_start(c, n_ctx, n_all, q, rev)
            lb = lb_ref[d:d + 1, :]
            qq = _silu(q_ref[pl.ds(s, q), :].astype(f32))
            f = lb + (1.0 - lb) * _sigmoid(f_ref[pl.ds(s, q), :].astype(f32))
            kk = 1.0 - f
            v = v_ref[pl.ds(s, q), :].astype(f32)
            cum = _dot_exact(tri_f, jnp.log(f))
            cl = cum[last:last + 1, :]
            st = st_ref[...]
            o = _dot_nt((qq * jnp.exp(cum)).astype(bf16), st.astype(bf16))
            att = jnp.zeros((q, q), f32)
            for sz, mask in zip(levels, masks):
                pieces = []
                for b0 in range(0, q, 2 * sz):
                    rr = b0 + sz if rev else b0 + sz - 1
                    pieces.append(jnp.broadcast_to(cum[rr:rr + 1, :], (2 * sz, cum.shape[1])))
                e = jnp.exp(-jnp.abs(cum - jnp.concatenate(pieces, axis=0)))
                att = att + jnp.where(mask, _dot_nt((qq * e).astype(bf16), (kk * e).astype(bf16)), 0.0)
            o = o + _dot(att.astype(bf16), v.astype(bf16))
            o = o + jnp.sum(qq * kk, axis=1, keepdims=True) * v
            for dd in range(1, HG_DIAG):
                sh = q - dd if rev else dd
                ks, cs, vs = pltpu.roll(kk, sh, 0), pltpu.roll(cum, sh, 0), pltpu.roll(v, sh, 0)
                ok = (r8 <= HG_DIAG - 1 - dd) if rev else (r8 >= dd)
                pr = jnp.where(ok, qq * ks * jnp.exp(cum - cs), 0.0)
                o = o + jnp.sum(pr, axis=1, keepdims=True) * vs
            st_ref[...] = st * jnp.exp(cl) + _dot_tn(v.astype(bf16), (kk * jnp.exp(cl - cum)).astype(bf16))
            if rev:
                y = _rms(acc_ref[pl.ds(s, q), :] + o, ng_ref[...])
                o_ref[pl.ds(s, q), :] = (y * _sigmoid(g_ref[pl.ds(s, q), :].astype(f32))).astype(o_ref.dtype)
            else:
                acc_ref[pl.ds(s, q), :] = o
            return carry

        lax.fori_loop(0, n_all, chunk, 0, unroll=2)


def _hgrn(p3, lb, norm_g, LC, LL):
    b, s, _ = p3.shape
    dk = HG_KEY_DIM
    n = HG_HEADS * dk

    def col(name, extra=0):
        base = _DST[name] // dk + extra
        return pl.BlockSpec((None, s, dk), lambda i, h: (i, 0, base + h))

    return pl.pallas_call(
        functools.partial(_hgrn_kernel, LC=LC, LL=LL), grid=(b, HG_HEADS),
        in_specs=[col('hg_q'), col('hg_f'), col('hg_f', HG_HEADS), col('hg_i'), col('hg_g'),
                  pl.BlockSpec((2, dk), lambda i, h: (0, h)),
                  pl.BlockSpec((1, dk), lambda i, h: (0, h))],
        out_specs=pl.BlockSpec((None, s, dk), lambda i, h: (i, 0, h)),
        out_shape=jax.ShapeDtypeStruct((b, s, n), bf16),
        scratch_shapes=[pltpu.VMEM((s, dk), f32), pltpu.VMEM((HG_VAL_DIM, dk), f32)],
        compiler_params=_cp("parallel", "parallel"), name="hgrn",
    )(p3, p3, p3, p3, p3, lb, norm_g.reshape(1, n))


def _mla_prep_kernel(cq_ref, ckv_ref, misc_ref, qg_ref, kvg_ref, wq_ref, wkv_ref, cos_ref, sin_ref,
                     q_out, k_out, v_out):
    h = MLA_HEADS
    scale = (MLA_NOPE + MLA_ROPE) ** -0.5
    cos, sin = cos_ref[...], sin_ref[...]
    qf = _dot(_rms(cq_ref[...].astype(f32), qg_ref[...]).astype(bf16), wq_ref[...]) * scale
    kv = _dot(_rms(ckv_ref[...].astype(f32), kvg_ref[...]).astype(bf16), wkv_ref[...])
    kr = _rope(misc_ref[...].astype(f32), cos, sin)[:, 0:MLA_ROPE].astype(bf16)
    ro = h * MLA_NOPE
    for pp in range(h * MLA_ROPE // LANE):
        qr = _rope(qf[:, ro + pp * LANE:ro + (pp + 1) * LANE], cos, sin).astype(bf16)
        for e in range(LANE // MLA_ROPE):
            q_out[pp * (LANE // MLA_ROPE) + e, :, MLA_NOPE:MLA_NOPE + MLA_ROPE] = qr[:, e * MLA_ROPE:(e + 1) * MLA_ROPE]
    for hh in range(h):
        q_out[hh, :, 0:MLA_NOPE] = qf[:, hh * MLA_NOPE:(hh + 1) * MLA_NOPE].astype(bf16)
        k_out[hh, :, 0:MLA_NOPE] = kv[:, hh * MLA_NOPE:(hh + 1) * MLA_NOPE].astype(bf16)
        k_out[hh, :, MLA_NOPE:MLA_NOPE + MLA_ROPE] = kr
        v_out[hh] = kv[:, ro + hh * MLA_V:ro + (hh + 1) * MLA_V].astype(bf16)


def _mla_prep(p, q_norm_g, kv_norm_g, w_uq_p, w_ukv_p, cos_s, sin_s, S):
    t = p.shape[0]
    tm = ROW_TILE
    ns = S // tm
    h, qd = MLA_HEADS, MLA_NOPE + MLA_ROPE
    return pl.pallas_call(
        _mla_prep_kernel, grid=(t // tm,),
        in_specs=[pl.BlockSpec((tm, MLA_Q_RANK), lambda i: (i, _DST['mla_cq'] // MLA_Q_RANK)),
                  pl.BlockSpec((tm, MLA_KV_RANK), lambda i: (i, _DST['mla_ckv'] // MLA_KV_RANK)),
                  pl.BlockSpec((tm, LANE), lambda i: (i, _MISC // LANE)),
                  pl.BlockSpec((1, MLA_Q_RANK), lambda i: (0, 0)),
                  pl.BlockSpec((1, MLA_KV_RANK), lambda i: (0, 0)),
                  pl.BlockSpec(w_uq_p.shape, lambda i: (0, 0)),
                  pl.BlockSpec(w_ukv_p.shape, lambda i: (0, 0)),
                  pl.BlockSpec((tm, LANE), lambda i: (i % ns, 0)),
                  pl.BlockSpec((tm, LANE), lambda i: (i % ns, 0))],
        out_specs=[pl.BlockSpec((h, tm, qd), lambda i: (0, i, 0)),
                   pl.BlockSpec((h, tm, qd), lambda i: (0, i, 0)),
                   pl.BlockSpec((h, tm, MLA_V), lambda i: (0, i, 0))],
        out_shape=[jax.ShapeDtypeStruct((h, t, qd), bf16), jax.ShapeDtypeStruct((h, t, qd), bf16),
                   jax.ShapeDtypeStruct((h, t, MLA_V), bf16)],
        compiler_params=_cp("parallel"), name="mla_prep",
    )(p, p, p, q_norm_g.reshape(1, -1), kv_norm_g.reshape(1, -1), w_uq_p, w_ukv_p, cos_s, sin_s)


def _mla_attn_kernel(q_ref, k_ref, v_ref, o_ref, *, LC, j0, tq):
    jb = pl.program_id(2) + j0
    s = _dot_nt(q_ref[...], k_ref[...])
    key = lax.broadcasted_iota(jnp.int32, (1, s.shape[1]), 1)
    s = jnp.where((jb < LC // tq) & (key >= LC), -1e30, s)
    e = jnp.exp(s - jnp.max(s, axis=1, keepdims=True))
    o = _dot(e.astype(bf16), v_ref[...]) / jnp.sum(e, axis=1, keepdims=True)
    o_ref[...] = o.astype(o_ref.dtype)


def _mla_attn(qh, kh, vh, B, LC, LL, lat_only):
    h, t, qd = qh.shape
    s = LC + LL
    tq = ROW_TILE
    j0 = LC // tq if lat_only else 0
    nq = s // tq - j0
    ns = s // tq
    return pl.pallas_call(
        functools.partial(_mla_attn_kernel, LC=LC, j0=j0, tq=tq), grid=(B, h, nq),
        in_specs=[pl.BlockSpec((None, tq, qd), lambda b, hh, j: (hh, b * ns + j + j0, 0)),
                  pl.BlockSpec((None, s, qd), lambda b, hh, j: (hh, b, 0)),
                  pl.BlockSpec((None, s, MLA_V), lambda b, hh, j: (hh, b, 0))],
        out_specs=pl.BlockSpec((tq, MLA_V), lambda b, hh, j: (b * ns + j + j0, hh)),
        out_shape=jax.ShapeDtypeStruct((t, h * MLA_V), bf16),
        compiler_params=_cp("parallel", "parallel", "parallel"), name="mla_attn",
    )(qh, kh, vh)


def _merge_kernel(y0, y1, y2, y3, g0, g1, g2, g3, w_ref, o_ref):
    acc = None
    for n, (y_ref, g_ref) in enumerate(((y0, g0), (y1, g1), (y2, g2), (y3, g3))):
        t = _sigmoid(g_ref[...].astype(f32)) * _dot(y_ref[...], w_ref[n])
        acc = t if acc is None else acc + t
    o_ref[...] = acc.astype(o_ref.dtype)


def _merge(ys, p, w_branch_b, rows):
    t = p.shape[0]
    d = D_MODEL
    tm, tn = rows.tm, 512
    gb = _DST['gates'] // tn
    y_specs = [pl.BlockSpec((tm, BRANCH_W), lambda i, j: (rows.src(i), 0)) for _ in range(N_BRANCH)]
    g_specs = [pl.BlockSpec((tm, tn), functools.partial(lambda i, j, n: (rows.src(i), gb + n * (d // tn) + j), n=n))
               for n in range(N_BRANCH)]
    return pl.pallas_call(
        _merge_kernel, grid=(rows.n, d // tn),
        in_specs=y_specs + g_specs + [pl.BlockSpec((N_BRANCH, BRANCH_W, tn), lambda i, j: (0, 0, j))],
        out_specs=pl.BlockSpec((tm, tn), lambda i, j: (i, j)),
        out_shape=jax.ShapeDtypeStruct((rows.n * tm, d), bf16),
        compiler_params=_cp("parallel", "parallel"), name="merge",
    )(*ys, p, p, p, p, w_branch_b)


def _out_proj_kernel(m_ref, w_ref, x_ref, mod_ref, gpost_ref, gffn_ref, rw_ref, x_out, h_out, l_out):
    r = _rms(_dot(m_ref[...], w_ref[...]), gpost_ref[...])
    x = x_ref[...] + mod_ref[2:3, :] * r
    x_out[...] = x
    h2 = _rms(x, gffn_ref[...]) * (1.0 + mod_ref[4:5, :]) + mod_ref[3:4, :]
    h_out[...] = h2.astype(h_out.dtype)
    l_out[...] = _dot_exact(h2, rw_ref[...])


def _out_proj(m, w_out_b, x, mods, g_post, g_ffn, router_w, rows):
    d = D_MODEL
    tm = rows.tm
    n = rows.n * tm
    ne = router_w.shape[1]
    return pl.pallas_call(
        _out_proj_kernel, grid=(rows.n,),
        in_specs=[pl.BlockSpec((tm, d), lambda i: (i, 0)),
                  pl.BlockSpec((d, d), lambda i: (0, 0)),
                  pl.BlockSpec((tm, d), lambda i: (rows.src(i), 0)),
                  pl.BlockSpec((None, 6, d), lambda i: (rows.cond(i), 0, 0)),
                  pl.BlockSpec((1, d), lambda i: (0, 0)),
                  pl.BlockSpec((1, d), lambda i: (0, 0)),
                  pl.BlockSpec((d, ne), lambda i: (0, 0))],
        out_specs=[pl.BlockSpec((tm, d), lambda i: (i, 0)),
                   pl.BlockSpec((tm, d), lambda i: (i, 0)),
                   pl.BlockSpec((tm, ne), lambda i: (i, 0))],
        out_shape=[jax.ShapeDtypeStruct((n, d), f32), jax.ShapeDtypeStruct((2 * n, d), bf16),
                   jax.ShapeDtypeStruct((n, ne), f32)],
        compiler_params=_cp("parallel"), name="out_proj",
    )(m, w_out_b, x, mods, g_post.reshape(1, d), g_ffn.reshape(1, d), router_w)


def _expert_kernel(be_ref, nu_ref, x_ref, w_ref, wg_ref, wu_ref, wd_ref, o_ref, wg_b, wu_b, wd_b):
    i = pl.program_id(0)

    @pl.when((i == 0) | (be_ref[i] != be_ref[jnp.maximum(i - 1, 0)]))
    def _():
        wg_b[...] = wg_ref[...].astype(bf16)
        wu_b[...] = wu_ref[...].astype(bf16)
        wd_b[...] = wd_ref[...].astype(bf16)

    @pl.when(i < nu_ref[0])
    def _():
        x = x_ref[...]
        hmid = _silu(_dot(x, wg_b[...])) * _dot(x, wu_b[...])
        o_ref[...] = (_dot(hmid.astype(bf16), wd_b[...]) * w_ref[...]).astype(o_ref.dtype)

    @pl.when(i >= nu_ref[0])
    def _():
        o_ref[...] = jnp.zeros_like(o_ref)


def _experts(x_sorted, row_w, blk_e, n_used, wg, wu, wd, layer):
    n_rows, d = x_sorted.shape
    ff = wg.shape[3]
    bm = MOE_ROWS
    grid_spec = pltpu.PrefetchScalarGridSpec(
        num_scalar_prefetch=2, grid=(n_rows // bm,),
        in_specs=[pl.BlockSpec((bm, d), lambda i, be, nu: (i, 0)),
                  pl.BlockSpec((bm, 1), lambda i, be, nu: (i, 0)),
                  pl.BlockSpec((None, None, d, ff), lambda i, be, nu: (layer, be[i], 0, 0)),
                  pl.BlockSpec((None, None, d, ff), lambda i, be, nu: (layer, be[i], 0, 0)),
                  pl.BlockSpec((None, None, ff, d), lambda i, be, nu: (layer, be[i], 0, 0))],
        out_specs=pl.BlockSpec((bm, d), lambda i, be, nu: (i, 0)),
        scratch_shapes=[pltpu.VMEM((d, ff), bf16), pltpu.VMEM((d, ff), bf16), pltpu.VMEM((ff, d), bf16)])
    return pl.pallas_call(
        _expert_kernel, grid_spec=grid_spec,
        out_shape=jax.ShapeDtypeStruct((n_rows, d), bf16),
        compiler_params=_cp("arbitrary"), name="experts",
    )(blk_e, n_used, x_sorted, row_w, wg, wu, wd)


def _ffn_out_kernel(h_ref, r_ref, x_ref, mod_ref, g_ref, sg_ref, su_ref, sd_ref, o_ref):
    h = h_ref[...]
    mid = _silu(_dot(h, sg_ref[...])) * _dot(h, su_ref[...])
    f = _dot(mid.astype(bf16), sd_ref[...]) + r_ref[...]
    o_ref[...] = x_ref[...] + mod_ref[5:6, :] * _rms(f, g_ref[...])


def _ffn_out(h2, routed, x, mods, g_post, sg, su, sd, rows):
    n, d = routed.shape
    tm = rows.tm
    ff = sg.shape[1]
    return pl.pallas_call(
        _ffn_out_kernel, grid=(rows.n,),
        in_specs=[pl.BlockSpec((tm, d), lambda i: (i, 0)),
                  pl.BlockSpec((tm, d), lambda i: (i, 0)),
                  pl.BlockSpec((tm, d), lambda i: (i, 0)),
                  pl.BlockSpec((None, 6, d), lambda i: (rows.cond(i), 0, 0)),
                  pl.BlockSpec((1, d), lambda i: (0, 0)),
                  pl.BlockSpec((d, ff), lambda i: (0, 0)),
                  pl.BlockSpec((d, ff), lambda i: (0, 0)),
                  pl.BlockSpec((ff, d), lambda i: (0, 0))],
        out_specs=pl.BlockSpec((tm, d), lambda i: (i, 0)),
        out_shape=jax.ShapeDtypeStruct((n, d), f32),
        compiler_params=_cp("parallel"), name="ffn_out",
    )(h2, routed, x, mods, g_post.reshape(1, d), sg, su, sd)


def _route_kernel(l_ref, b_ref, e_out, w_out):
    scores = _sigmoid(l_ref[...])
    sel = scores + b_ref[...]
    tm, ne = sel.shape
    per_group = ne // N_EXPERT_GROUPS
    lane = lax.broadcasted_iota(jnp.int32, (tm, ne), 1)
    lane_f = lane.astype(f32)
    grp = lane // per_group
    neg = -jnp.inf

    def first_max(v):
        m = jnp.max(v, axis=1, keepdims=True)
        return m, jnp.min(jnp.where(v == m, lane_f, float(ne)), axis=1, keepdims=True)

    gs = []
    for g in range(N_EXPERT_GROUPS):
        sg = jnp.where(grp == g, sel, neg)
        m1, i1 = first_max(sg)
        gs.append(m1 + jnp.max(jnp.where(lane_f == i1, neg, sg), axis=1, keepdims=True))
    chosen = [jnp.zeros((tm, 1), jnp.bool_)] * N_EXPERT_GROUPS
    for _ in range(TOPK_EXPERT_GROUPS):
        m = functools.reduce(jnp.maximum, gs)
        found = jnp.zeros((tm, 1), jnp.bool_)
        for g in range(N_EXPERT_GROUPS):
            hit = (gs[g] == m) & ~found
            chosen[g] = chosen[g] | hit
            found = found | hit
            gs[g] = jnp.where(hit, neg, gs[g])
    keep = jnp.zeros((tm, ne), jnp.bool_)
    for g in range(N_EXPERT_GROUPS):
        keep = keep | ((grp == g) & chosen[g])
    cur = jnp.where(keep, sel, neg)
    ids, wts = [], []
    for _ in range(TOP_K):
        _, idx = first_max(cur)
        hit = lane_f == idx
        ids.append(idx)
        wts.append(jnp.sum(jnp.where(hit, scores, 0.0), axis=1, keepdims=True))
        cur = jnp.where(hit, neg, cur)
    norm = ROUTE_SCALE / functools.reduce(lambda a, b: a + b, wts)
    out_lane = lax.broadcasted_iota(jnp.int32, (tm, LANE), 1)
    e = jnp.zeros((tm, LANE), f32)
    w = jnp.zeros((tm, LANE), f32)
    for k in range(TOP_K):
        e = jnp.where(out_lane == k, ids[k], e)
        w = jnp.where(out_lane == k, wts[k] * norm, w)
    e_out[...] = e.astype(jnp.int32)
    w_out[...] = w


def _route(logits, router_bias):
    t, ne = logits.shape
    tm = ROW_TILE
    e, w = pl.pallas_call(
        _route_kernel, grid=(t // tm,),
        in_specs=[pl.BlockSpec((tm, ne), lambda i: (i, 0)), pl.BlockSpec((1, ne), lambda i: (0, 0))],
        out_specs=[pl.BlockSpec((tm, LANE), lambda i: (i, 0)), pl.BlockSpec((tm, LANE), lambda i: (i, 0))],
        out_shape=[jax.ShapeDtypeStruct((t, LANE), jnp.int32), jax.ShapeDtypeStruct((t, LANE), f32)],
        compiler_params=_cp("parallel"), name="route",
    )(logits, router_bias.astype(f32).reshape(1, ne))
    return e[:, :TOP_K], w[:, :TOP_K]


def _moe(h2, logits, router_bias, wg, wu, wd, layer):
    t, d = logits.shape[0], h2.shape[1]
    top_e, top_w = _route(logits, router_bias)
    n_assign = t * TOP_K
    flat_e = top_e.reshape(n_assign)
    onehot = (flat_e[:, None] == jnp.arange(N_EXPERTS)[None, :]).astype(jnp.int32)
    csum = jnp.cumsum(onehot, axis=0)
    counts = csum[-1]
    rank = jnp.take_along_axis(csum, flat_e[:, None], axis=1)[:, 0] - 1
    padded = (counts + MOE_ROWS - 1) // MOE_ROWS * MOE_ROWS
    pad_end = jnp.cumsum(padded)
    dest = (pad_end - padded)[flat_e] + rank
    n_blocks = -(-n_assign // MOE_ROWS) + N_EXPERTS
    n_rows = n_blocks * MOE_ROWS
    row_tok = jnp.zeros((n_rows,), jnp.int32).at[dest].set(jnp.arange(n_assign, dtype=jnp.int32) // TOP_K)
    row_w = jnp.zeros((n_rows,), f32).at[dest].set(top_w.reshape(n_assign))
    blk_e = jnp.minimum(jnp.searchsorted(pad_end, jnp.arange(n_blocks) * MOE_ROWS, side='right'),
                        N_EXPERTS - 1).astype(jnp.int32)
    n_used = (pad_end[-1] // MOE_ROWS).astype(jnp.int32).reshape(1)
    x_sorted = h2.at[row_tok].get(mode="promise_in_bounds")
    y_sorted = _experts(x_sorted, row_w.reshape(n_rows, 1), blk_e, n_used, wg, wu, wd, layer)
    dest_k = dest.reshape(t, TOP_K).T.reshape(n_assign)
    return y_sorted.at[dest_k].get(mode="promise_in_bounds").reshape(TOP_K, t, d).astype(f32).sum(axis=0)


def _permute_w_in(w_in):
    cols = [w_in[:, _SRC[name][0]:_SRC[name][0] + _SRC[name][1]] for name in _P_ORDER]
    cols.append(jnp.zeros((w_in.shape[0], _P_WIDTH - _P_USED), w_in.dtype))
    return jnp.concatenate(cols, axis=1).astype(bf16)


def _permute_mla(w_uq, w_ukv):
    h = MLA_HEADS
    q = w_uq.reshape(-1, h, MLA_NOPE + MLA_ROPE)
    w_uq_p = jnp.concatenate([q[:, :, :MLA_NOPE].reshape(-1, h * MLA_NOPE), q[:, :, MLA_NOPE:].reshape(-1, h * MLA_ROPE)], 1)
    kv = w_ukv.reshape(-1, h, MLA_NOPE + MLA_V)
    w_ukv_p = jnp.concatenate([kv[:, :, :MLA_NOPE].reshape(-1, h * MLA_NOPE), kv[:, :, MLA_NOPE:].reshape(-1, h * MLA_V)], 1)
    return w_uq_p.astype(bf16), w_ukv_p.astype(bf16)


def _rope_tables(LC, LL):
    pos = jnp.arange(LL)
    n_freq = SWA_HEAD_DIM // 4
    inv_freq = ROPE_BASE ** (-jnp.arange(n_freq, dtype=f32) / n_freq)
    ang = jnp.concatenate([(pos // GRID_W).astype(f32)[:, None] * inv_freq,
                           (pos % GRID_W).astype(f32)[:, None] * inv_freq], axis=-1)
    cos, sin = jnp.cos(ang), jnp.sin(ang)
    cos = jnp.concatenate([jnp.ones((LC, cos.shape[1]), f32), cos], axis=0)
    sin = jnp.concatenate([jnp.zeros((LC, sin.shape[1]), f32), sin], axis=0)
    return jnp.tile(jnp.concatenate([cos, cos], 1), (1, 2)), jnp.tile(jnp.concatenate([-sin, sin], 1), (1, 2))


def _dir_lanes(v):
    out = jnp.zeros((2, LANE), f32)
    for d in range(2):
        out = out.at[d, _DT_LANE + d * SSD_HEADS:_DT_LANE + (d + 1) * SSD_HEADS].set(v[d].astype(f32))
    return out


def kernel(x, c, ctx, c_ctx, w_mod, b_mod, norm_pre_mix, norm_post_mix, norm_pre_ffn, norm_post_ffn, w_in, ssd_conv_w, ssd_conv_b, ssd_dt_bias, ssd_a_log, ssd_d, ssd_norm, swa_sink, hg_lb_logits, hg_norm, mla_q_norm, mla_kv_norm, mla_w_uq, mla_w_ukv, w_branch, w_out, router_w, router_bias, expert_w_gate, expert_w_up, expert_w_down, shared_w_gate, shared_w_up, shared_w_down):
    B, LL, d = x.shape
    LC = ctx.shape[1]
    S = LC + LL
    depth = w_mod.shape[0]
    assert LC % ROW_TILE == 0 and LL % ROW_TILE == 0 and LL >= 3 * WINDOW and d == D_MODEL

    n_cond = -(-(B + 1) // 8) * 8
    cond = jnp.zeros((n_cond, d), f32).at[:B].set(c).at[B].set(c_ctx)
    mods_all = _modulation(cond, w_mod, b_mod).reshape(depth, n_cond, 6, d)
    cos_s, sin_s = _rope_tables(LC, LL)
    sm = jax.nn.softmax(hg_lb_logits.astype(f32), axis=0)
    lower_bounds = jnp.cumsum(sm, axis=0) - sm[0]

    xs = jnp.concatenate([ctx, x], axis=1).reshape(B * S, d)
    for l in range(depth):
        last = l == depth - 1
        mods = mods_all[l]
        all_rows = _Rows(B, LC, LL, False)
        out_rows = _Rows(B, LC, LL, last)
        h = _norm_mod(xs, norm_pre_mix[l], mods, all_rows)
        p = _matmul(h, _permute_w_in(w_in[l]), 1024, _P_TN, bf16, "in_proj")
        p3 = p.reshape(B, S, _P_WIDTH)

        xbc3 = _ssd_conv(p3, ssd_conv_w[l], ssd_conv_b[l], LC, LL)
        y_ssd = _ssd_scan(xbc3, p3, _dir_lanes(ssd_dt_bias[l]), _dir_lanes(-jnp.exp(ssd_a_log[l].astype(f32))), LC, LL)
        d_in = SSD_HEADS * SSD_HEAD_DIM
        y_ssd = _ssd_finish(y_ssd.reshape(B * S, d_in), xbc3.reshape(B * S, -1), p,
                            jnp.repeat(ssd_d[l].astype(f32), SSD_HEAD_DIM).reshape(1, d_in), ssd_norm[l])
        y_swa = _swa(p3, swa_sink[l].astype(f32), cos_s, sin_s, LC, LL, last).reshape(B * S, -1)
        y_hg = _hgrn(p3, lower_bounds[l], hg_norm[l], LC, LL).reshape(B * S, -1)
        w_uq_p, w_ukv_p = _permute_mla(mla_w_uq[l], mla_w_ukv[l])
        qh, kh, vh = _mla_prep(p, mla_q_norm[l], mla_kv_norm[l], w_uq_p, w_ukv_p, cos_s, sin_s, S)
        y_mla = _mla_attn(qh, kh, vh, B, LC, LL, last)

        m = _merge((y_ssd, y_swa, y_hg, y_mla), p, w_branch[l].astype(bf16), out_rows)
        x_mid, h2, logits = _out_proj(m, w_out[l].astype(bf16), xs, mods, norm_post_mix[l], norm_pre_ffn[l],
                                      router_w[l], out_rows)
        routed = _moe(h2, logits, router_bias[l], expert_w_gate, expert_w_up, expert_w_down, l)
        xs = _ffn_out(h2, routed, x_mid, mods, norm_post_ffn[l], shared_w_gate[l].astype(bf16),
                      shared_w_up[l].astype(bf16), shared_w_down[l].astype(bf16), out_rows)
    return xs.reshape(B, LL, d)
```

```python
import functools
import math

import jax
import jax.numpy as jnp
import numpy as np
from jax import lax
from jax.experimental import pallas as pl
from jax.experimental.pallas import tpu as pltpu

f32 = jnp.float32
bf16 = jnp.bfloat16

D_MODEL = 2048
GRID_W = 64
N_BRANCH = 4
BRANCH_W = 1024
EPS = 1e-6
ROPE_BASE = 10000.0

SSD_HEADS = 16
SSD_HEAD_DIM = 64
SSD_GROUPS = 4
SSD_STATE = 128
SSD_CONV = 5
SSD_CHUNK = 128

SWA_HEADS = 16
SWA_KV_HEADS = 4
SWA_HEAD_DIM = 64
WINDOW = 128

HG_HEADS = 8
HG_KEY_DIM = 128
HG_VAL_DIM = 128
HG_CHUNK = 64
HG_DIAG = 8

MLA_HEADS = 8
MLA_Q_RANK = 512
MLA_KV_RANK = 256
MLA_NOPE = 128
MLA_ROPE = 64
MLA_V = 128

N_EXPERTS = 64
TOP_K = 6
N_EXPERT_GROUPS = 8
TOPK_EXPERT_GROUPS = 4
EXPERT_FF = 512
SHARED_FF = 1024
ROUTE_SCALE = 2.5
MOE_ROWS = 256

LANE = 128
ROW_TILE = 256
VMEM_LIMIT = 48 * 1024 * 1024

_IN_LAYOUT = (
    ('ssd_z', 1024), ('ssd_x', 1024), ('ssd_B', 512), ('ssd_C', 512), ('ssd_dt', 32),
    ('swa_q', 1024), ('swa_k', 256), ('swa_v', 256),
    ('hg_q', 1024), ('hg_f', 2048), ('hg_i', 1024), ('hg_g', 1024),
    ('mla_cq', 512), ('mla_ckv', 256), ('mla_kr', 64), ('gates', 8192),
)
_P_ORDER = ('gates', 'hg_f', 'ssd_x', 'ssd_B', 'ssd_C', 'ssd_z', 'swa_q', 'hg_q', 'hg_i', 'hg_g',
            'mla_cq', 'swa_k', 'swa_v', 'mla_ckv', 'mla_kr', 'ssd_dt')
_P_TN = 512


def _layout():
    src, off = {}, 0
    for name, size in _IN_LAYOUT:
        src[name] = (off, size)
        off += size
    dst, off = {}, 0
    for name in _P_ORDER:
        dst[name] = off
        off += src[name][1]
    width = -(-off // _P_TN) * _P_TN
    return src, dst, off, width


_SRC, _DST, _P_USED, _P_WIDTH = _layout()
_MISC = _DST['mla_kr']
_DT_LANE = _DST['ssd_dt'] - _MISC


def _cp(*sem):
    return pltpu.CompilerParams(dimension_semantics=sem, vmem_limit_bytes=VMEM_LIMIT)


def _sigmoid(x):
    return 1.0 / (1.0 + jnp.exp(-x))


def _silu(x):
    return x * _sigmoid(x)


def _rms(x, gain):
    return x * lax.rsqrt(jnp.mean(x * x, axis=-1, keepdims=True) + EPS) * gain


def _dot(a, b):
    return jnp.dot(a, b, preferred_element_type=f32)


def _dot_nt(a, b):
    return lax.dot_general(a, b, (((1,), (1,)), ((), ())), preferred_element_type=f32)


def _dot_tn(a, b):
    return lax.dot_general(a, b, (((0,), (0,)), ((), ())), preferred_element_type=f32)


def _dot_exact(a, b):
    return jnp.dot(a, b, preferred_element_type=f32, precision=lax.Precision.HIGHEST)


class _Rows:
    def __init__(self, B, LC, LL, lat_only, tm=ROW_TILE):
        self.B, self.LC, self.LL, self.tm = B, LC, LL, tm
        self.nC, self.nL = LC // tm, LL // tm
        self.nS = self.nC + self.nL
        self.lat_only = lat_only
        self.n = B * (self.nL if lat_only else self.nS)

    def src(self, i):
        if self.lat_only:
            return (i // self.nL) * self.nS + self.nC + i % self.nL
        return i

    def cond(self, i):
        if self.lat_only:
            return i // self.nL
        return jnp.where(i % self.nS < self.nC, self.B, i // self.nS)

    def pos(self, i):
        if self.lat_only:
            return self.nC + i % self.nL
        return i % self.nS


def _mod_kernel(c_ref, w_ref, b_ref, o_ref):
    s = _silu(c_ref[...]).astype(bf16)
    o_ref[...] = _dot(s, w_ref[...].astype(bf16)) + b_ref[...]


def _modulation(cond, w_mod, b_mod):
    nl, d, n = w_mod.shape
    nc = cond.shape[0]
    tn = 1024
    return pl.pallas_call(
        _mod_kernel, grid=(nl, n // tn),
        in_specs=[pl.BlockSpec((nc, d), lambda l, j: (0, 0)),
                  pl.BlockSpec((None, d, tn), lambda l, j: (l, 0, j)),
                  pl.BlockSpec((None, 1, tn), lambda l, j: (l, 0, j))],
        out_specs=pl.BlockSpec((None, nc, tn), lambda l, j: (l, 0, j)),
        out_shape=jax.ShapeDtypeStruct((nl, nc, n), f32),
        compiler_params=_cp("parallel", "parallel"), name="modulation",
    )(cond, w_mod, b_mod.reshape(nl, 1, n))


def _norm_mod_kernel(x_ref, g_ref, mod_ref, o_ref):
    y = _rms(x_ref[...], g_ref[...])
    o_ref[...] = (y * (1.0 + mod_ref[1:2, :]) + mod_ref[0:1, :]).astype(o_ref.dtype)


def _norm_mod(x, gain, mods, rows):
    t, d = x.shape
    tm = rows.tm
    return pl.pallas_call(
        _norm_mod_kernel, grid=(rows.n,),
        in_specs=[pl.BlockSpec((tm, d), lambda i: (rows.src(i), 0)),
                  pl.BlockSpec((1, d), lambda i: (0, 0)),
                  pl.BlockSpec((None, 6, d), lambda i: (rows.cond(i), 0, 0))],
        out_specs=pl.BlockSpec((tm, d), lambda i: (rows.src(i), 0)),
        out_shape=jax.ShapeDtypeStruct((t, d), bf16),
        compiler_params=_cp("parallel"), name="norm_mod",
    )(x, gain.reshape(1, d), mods)


def _mm_kernel(a_ref, b_ref, o_ref):
    o_ref[...] = _dot(a_ref[...], b_ref[...]).astype(o_ref.dtype)


def _matmul(a, b, tm, tn, out_dtype, name):
    m, k = a.shape
    n = b.shape[1]
    return pl.pallas_call(
        _mm_kernel, grid=(m // tm, n // tn),
        in_specs=[pl.BlockSpec((tm, k), lambda i, j: (i, 0)),
                  pl.BlockSpec((k, tn), lambda i, j: (0, j))],
        out_specs=pl.BlockSpec((tm, tn), lambda i, j: (i, j)),
        out_shape=jax.ShapeDtypeStruct((m, n), out_dtype),
        compiler_params=_cp("parallel", "parallel"), name=name,
    )(a, b)


def _ssd_conv_kernel(x_ref, w_ref, b_ref, o_ref, *, LC, LL):
    s = LC + LL
    x = x_ref[...].astype(f32)
    row = lax.broadcasted_iota(jnp.int32, (s, 1), 0)
    pos = jnp.where(row < LC, row, row - LC)
    length = jnp.where(row < LC, LC, LL)
    acc = jnp.zeros_like(x) + b_ref[...]
    half = SSD_CONV // 2
    for k in range(SSD_CONV):
        sh = k - half
        xs = x if sh == 0 else pltpu.roll(x, (-sh) % s, 0)
        valid = (pos + sh >= 0) & (pos + sh < length)
        acc = acc + w_ref[k:k + 1, :] * jnp.where(valid, xs, 0.0)
    o_ref[...] = _silu(acc).astype(o_ref.dtype)


def _ssd_conv(p3, conv_w, conv_b, LC, LL):
    b, s, _ = p3.shape
    n = conv_w.shape[1]
    tc = 256
    base = _DST['ssd_x'] // tc
    return pl.pallas_call(
        functools.partial(_ssd_conv_kernel, LC=LC, LL=LL), grid=(b, n // tc),
        in_specs=[pl.BlockSpec((None, s, tc), lambda i, j: (i, 0, base + j)),
                  pl.BlockSpec((SSD_CONV, tc), lambda i, j: (0, j)),
                  pl.BlockSpec((1, tc), lambda i, j: (0, j))],
        out_specs=pl.BlockSpec((None, s, tc), lambda i, j: (i, 0, j)),
        out_shape=jax.ShapeDtypeStruct((b, s, n), bf16),
        compiler_params=_cp("parallel", "parallel"), name="ssd_conv",
    )(p3, conv_w, conv_b.reshape(1, n))


def _chunk_start(c, n_ctx, n_all, q, rev):
    if rev:
        c = jnp.where(c < n_ctx, n_ctx - 1 - c, n_all - 1 - (c - n_ctx))
    return pl.multiple_of(c * q, q)


def _pick4(hid, vals):
    return jnp.where(hid == 0, vals[0], jnp.where(hid == 1, vals[1], jnp.where(hid == 2, vals[2], vals[3])))


def _ssd_scan_kernel(x_ref, b_ref, c_ref, dt_ref, bias_ref, a_ref, o_ref, st_ref, *, LC, LL):
    q = SSD_CHUNK
    r = SSD_HEADS // SSD_GROUPS
    g = pl.program_id(1)
    n_ctx, n_all = LC // q, (LC + LL) // q
    ii = lax.broadcasted_iota(jnp.int32, (q, q), 0)
    jj = lax.broadcasted_iota(jnp.int32, (q, q), 1)
    lane = lax.broadcasted_iota(jnp.int32, (1, LANE), 1)
    sub = lax.broadcasted_iota(jnp.int32, (LANE, 1), 0)
    hid = lax.broadcasted_iota(jnp.int32, (1, r * SSD_HEAD_DIM), 1) // SSD_HEAD_DIM
    lane_lo = lane < SSD_HEAD_DIM

    for d in range(2):
        rev = d == 1
        tri = (ii <= jj) if rev else (ii >= jj)
        tri_f = tri.astype(f32)
        last = 0 if rev else q - 1
        st_ref[...] = jnp.zeros_like(st_ref)

        def chunk(c, carry, d=d, rev=rev, tri=tri, tri_f=tri_f, last=last):
            s = _chunk_start(c, n_ctx, n_all, q, rev)
            raw = dt_ref[pl.ds(s, q), :].astype(f32) + bias_ref[d:d + 1, :]
            dtv = jnp.maximum(raw, 0.0) + jnp.log1p(jnp.exp(-jnp.abs(raw)))
            a = dtv * a_ref[d:d + 1, :]
            acum = _dot_exact(tri_f, a)
            acum_t = acum.T
            x = x_ref[pl.ds(s, q), :].astype(f32)
            bm = b_ref[pl.ds(s, q), :]
            cm = c_ref[pl.ds(s, q), :]
            cb = _dot_nt(cm, bm)
            acol, arow, dcol, alast = [], [], [], []
            for hh in range(r):
                col = _DT_LANE + d * SSD_HEADS + g * r + hh
                acol.append(jnp.sum(jnp.where(lane == col, acum, 0.0), axis=1, keepdims=True))
                arow.append(jnp.sum(jnp.where(sub == col, acum_t, 0.0), axis=0, keepdims=True))
                dcol.append(jnp.sum(jnp.where(lane == col, dtv, 0.0), axis=1, keepdims=True))
                alast.append(acol[hh][last:last + 1, :])
            a_all = _pick4(hid, acol)
            al_all = _pick4(hid, alast)
            dtx = _pick4(hid, dcol) * x
            st = st_ref[...]
            y_off = _dot(cm, st.astype(bf16)) * jnp.exp(a_all)
            y_diag = []
            for pp in range(r // 2):
                decay = [jnp.exp(jnp.where(tri, acol[h] - arow[h], -1e30)) * cb for h in (2 * pp, 2 * pp + 1)]
                lhs = jnp.concatenate(decay, axis=1).astype(bf16)
                xp = dtx[:, pp * LANE:(pp + 1) * LANE]
                rhs = jnp.concatenate([jnp.where(lane_lo, xp, 0.0), jnp.where(lane_lo, 0.0, xp)], axis=0)
                y_diag.append(_dot(lhs, rhs.astype(bf16)))
            y = y_off + jnp.concatenate(y_diag, axis=1)
            if rev:
                o_ref[pl.ds(s, q), :] += y
            else:
                o_ref[pl.ds(s, q), :] = y
            w = (jnp.exp(al_all - a_all) * dtx).astype(bf16)
            st_ref[...] = st * jnp.exp(al_all) + _dot_tn(bm, w)
            return carry

        lax.fori_loop(0, n_all, chunk, 0, unroll=2)


def _ssd_scan(xbc3, p3, bias_v, a_v, LC, LL):
    b, s, _ = xbc3.shape
    d_in = SSD_HEADS * SSD_HEAD_DIM
    gw = d_in // SSD_GROUPS
    nb = d_in // SSD_STATE
    return pl.pallas_call(
        functools.partial(_ssd_scan_kernel, LC=LC, LL=LL), grid=(b, SSD_GROUPS),
        in_specs=[pl.BlockSpec((None, s, gw), lambda i, g: (i, 0, g)),
                  pl.BlockSpec((None, s, SSD_STATE), lambda i, g: (i, 0, nb + g)),
                  pl.BlockSpec((None, s, SSD_STATE), lambda i, g: (i, 0, nb + SSD_GROUPS + g)),
                  pl.BlockSpec((None, s, LANE), lambda i, g: (i, 0, _MISC // LANE)),
                  pl.BlockSpec((2, LANE), lambda i, g: (0, 0)),
                  pl.BlockSpec((2, LANE), lambda i, g: (0, 0))],
        out_specs=pl.BlockSpec((None, s, gw), lambda i, g: (i, 0, g)),
        out_shape=jax.ShapeDtypeStruct((b, s, d_in), f32),
        scratch_shapes=[pltpu.VMEM((SSD_STATE, gw), f32)],
        compiler_params=_cp("parallel", "parallel"), name="ssd_scan",
    )(xbc3, xbc3, xbc3, p3, bias_v, a_v)


def _ssd_finish_kernel(y_ref, x_ref, z_ref, dsk_ref, g_ref, o_ref):
    y = y_ref[...] + x_ref[...].astype(f32) * dsk_ref[...]
    o_ref[...] = _rms(y * _silu(z_ref[...].astype(f32)), g_ref[...]).astype(o_ref.dtype)


def _ssd_finish(y, xbc, p, d_skip_row, norm_g):
    t, n = y.shape
    tm = ROW_TILE
    zb = _DST['ssd_z'] // n
    return pl.pallas_call(
        _ssd_finish_kernel, grid=(t // tm,),
        in_specs=[pl.BlockSpec((tm, n), lambda i: (i, 0)),
                  pl.BlockSpec((tm, n), lambda i: (i, 0)),
                  pl.BlockSpec((tm, n), lambda i: (i, zb)),
                  pl.BlockSpec((1, n), lambda i: (0, 0)),
                  pl.BlockSpec((1, n), lambda i: (0, 0))],
        out_specs=pl.BlockSpec((tm, n), lambda i: (i, 0)),
        out_shape=jax.ShapeDtypeStruct((t, n), bf16),
        compiler_params=_cp("parallel"), name="ssd_finish",
    )(y, xbc, p, d_skip_row, norm_g.reshape(1, n))


def _rope(x, cos, sin):
    lane = lax.broadcasted_iota(jnp.int32, (1, LANE), 1)
    first = (lane % 64) < 32
    partner = jnp.where(first, pltpu.roll(x, LANE - 32, 1), pltpu.roll(x, 32, 1))
    return x * cos + partner * sin


def _swa_kernel(sink_ref, q_ref, k_ref, v_ref, cos_ref, sin_ref, o_ref, *, LC, LL, j0):
    w = WINDOW
    hd = SWA_HEAD_DIM
    grp = SWA_HEADS // SWA_KV_HEADS
    kv_w = SWA_KV_HEADS * hd
    jb = pl.program_id(1) + j0
    n_ctx = LC // w
    is_lat = jb >= n_ctx
    nb = jnp.maximum(jb - n_ctx, 0)
    qs = pl.multiple_of(jb * w, w)
    k0 = pl.multiple_of(jnp.clip((nb - 1) * w, 0, LL - 3 * w), w)
    ks = pl.multiple_of(LC + k0, w)
    scale = hd ** -0.5

    cos_q, sin_q = cos_ref[pl.ds(qs, w), :], sin_ref[pl.ds(qs, w), :]
    cos_k, sin_k = cos_ref[pl.ds(ks, 3 * w), :], sin_ref[pl.ds(ks, 3 * w), :]
    q = q_ref[...].astype(f32)
    q = jnp.concatenate([_rope(q[:, c * LANE:(c + 1) * LANE], cos_q, sin_q) for c in range(SWA_HEADS * hd // LANE)],
                        axis=1) * scale
    kb = k_ref[pl.ds(ks, 3 * w), :].astype(f32)
    kb = jnp.concatenate([_rope(kb[:, c * LANE:(c + 1) * LANE], cos_k, sin_k) for c in range(kv_w // LANE)],
                         axis=1).astype(bf16)
    vb = v_ref[pl.ds(ks, 3 * w), :]
    kc = k_ref[0:LC, :]
    vc = v_ref[0:LC, :]

    rows = grp * w
    q_abs = nb * w + lax.broadcasted_iota(jnp.int32, (rows, 1), 0) % w
    k_abs = k0 + lax.broadcasted_iota(jnp.int32, (1, 3 * w), 1)
    valid = is_lat & (jnp.abs(k_abs - q_abs) <= w)
    head_of_row = lax.broadcasted_iota(jnp.int32, (rows, 1), 0) // w

    outs = []
    for g in range(SWA_KV_HEADS):
        qg = jnp.concatenate([q[:, (g * grp + r) * hd:(g * grp + r + 1) * hd] for r in range(grp)], axis=0).astype(bf16)
        sl = slice(g * hd, (g + 1) * hd)
        s_lat = jnp.where(valid, _dot_nt(qg, kb[:, sl]), -1e30)
        s_ctx = _dot_nt(qg, kc[:, sl])
        sink = jnp.zeros((rows, 1), f32)
        for r in range(grp):
            sink = jnp.where(head_of_row == r, sink_ref[g * grp + r], sink)
        m = jnp.maximum(jnp.maximum(jnp.max(s_lat, axis=1, keepdims=True), jnp.max(s_ctx, axis=1, keepdims=True)), sink)
        e_lat = jnp.exp(s_lat - m)
        e_ctx = jnp.exp(s_ctx - m)
        den = jnp.sum(e_lat, axis=1, keepdims=True) + jnp.sum(e_ctx, axis=1, keepdims=True) + jnp.exp(sink - m)
        og = (_dot(e_lat.astype(bf16), vb[:, sl]) + _dot(e_ctx.astype(bf16), vc[:, sl])) / den
        outs.extend(og[r * w:(r + 1) * w, :] for r in range(grp))
    o_ref[...] = jnp.concatenate(outs, axis=1).astype(o_ref.dtype)


def _swa(p3, sink, cos_s, sin_s, LC, LL, lat_only):
    b, s, _ = p3.shape
    w = WINDOW
    j0 = LC // w if lat_only else 0
    nq = s // w - j0
    qw = SWA_HEADS * SWA_HEAD_DIM
    kw = SWA_KV_HEADS * SWA_HEAD_DIM
    return pl.pallas_call(
        functools.partial(_swa_kernel, LC=LC, LL=LL, j0=j0), grid=(b, nq),
        in_specs=[pl.BlockSpec(memory_space=pltpu.SMEM),
                  pl.BlockSpec((None, w, qw), lambda i, j: (i, j + j0, _DST['swa_q'] // qw)),
                  pl.BlockSpec((None, s, kw), lambda i, j: (i, 0, _DST['swa_k'] // kw)),
                  pl.BlockSpec((None, s, kw), lambda i, j: (i, 0, _DST['swa_v'] // kw)),
                  pl.BlockSpec((s, LANE), lambda i, j: (0, 0)),
                  pl.BlockSpec((s, LANE), lambda i, j: (0, 0))],
        out_specs=pl.BlockSpec((None, w, qw), lambda i, j: (i, j + j0, 0)),
        out_shape=jax.ShapeDtypeStruct((b, s, qw), bf16),
        compiler_params=_cp("parallel", "parallel"), name="swa",
    )(sink, p3, p3, p3, cos_s, sin_s)


def _hgrn_kernel(q_ref, ff_ref, fr_ref, v_ref, g_ref, lb_ref, ng_ref, o_ref, acc_ref, st_ref, *, LC, LL):
    q = HG_CHUNK
    n_ctx, n_all = LC // q, (LC + LL) // q
    ii = lax.broadcasted_iota(jnp.int32, (q, q), 0)
    jj = lax.broadcasted_iota(jnp.int32, (q, q), 1)
    r8 = lax.broadcasted_iota(jnp.int32, (q, 1), 0) % HG_DIAG
    levels = []
    sz = HG_DIAG
    while sz < q:
        levels.append(sz)
        sz *= 2

    for d in range(2):
        rev = d == 1
        tri_f = ((ii <= jj) if rev else (ii >= jj)).astype(f32)
        last = 0 if rev else q - 1
        f_ref = fr_ref if rev else ff_ref
        st_ref[...] = jnp.zeros_like(st_ref)
        masks = []
        for sz in levels:
            same = (ii // (2 * sz)) == (jj // (2 * sz))
            i_hi, j_hi = (ii % (2 * sz)) >= sz, (jj % (2 * sz)) >= sz
            masks.append(same & (~i_hi & j_hi if rev else i_hi & ~j_hi))

        def chunk(c, carry, d=d, rev=rev, tri_f=tri_f, last=last, f_ref=f_ref, masks=masks):
            s = _chunk_start(c, n_ctx, n_all, q, rev)
            lb = lb_ref[d:d + 1, :]
            qq = _silu(q_ref[pl.ds(s, q), :].astype(f32))
            f = lb + (1.0 - lb) * _sigmoid(f_ref[pl.ds(s, q), :].astype(f32))
            kk = 1.0 - f
            v = v_ref[pl.ds(s, q), :].astype(f32)
            cum = _dot_exact(tri_f, jnp.log(f))
            cl = cum[last:last + 1, :]
            st = st_ref[...]
            o = _dot_nt((qq * jnp.exp(cum)).astype(bf16), st.astype(bf16))
            att = jnp.zeros((q, q), f32)
            for sz, mask in zip(levels, masks):
                pieces = []
                for b0 in range(0, q, 2 * sz):
                    rr = b0 + sz if rev else b0 + sz - 1
                    pieces.append(jnp.broadcast_to(cum[rr:rr + 1, :], (2 * sz, cum.shape[1])))
                e = jnp.exp(-jnp.abs(cum - jnp.concatenate(pieces, axis=0)))
                att = att + jnp.where(mask, _dot_nt((qq * e).astype(bf16), (kk * e).astype(bf16)), 0.0)
            o = o + _dot(att.astype(bf16), v.astype(bf16))
            o = o + jnp.sum(qq * kk, axis=1, keepdims=True) * v
            for dd in range(1, HG_DIAG):
                sh = q - dd if rev else dd
                ks, cs, vs = pltpu.roll(kk, sh, 0), pltpu.roll(cum, sh, 0), pltpu.roll(v, sh, 0)
                ok = (r8 <= HG_DIAG - 1 - dd) if rev else (r8 >= dd)
                pr = jnp.where(ok, qq * ks * jnp.exp(cum - cs), 0.0)
                o = o + jnp.sum(pr, axis=1, keepdims=True) * vs
            st_ref[...] = st * jnp.exp(cl) + _dot_tn(v.astype(bf16), (kk * jnp.exp(cl - cum)).astype(bf16))
            if rev:
                y = _rms(acc_ref[pl.ds(s, q), :] + o, ng_ref[...])
                o_ref[pl.ds(s, q), :] = (y * _sigmoid(g_ref[pl.ds(s, q), :].astype(f32))).astype(o_ref.dtype)
            else:
                acc_ref[pl.ds(s, q), :] = o
            return carry

        lax.fori_loop(0, n_all, chunk, 0, unroll=4)


def _hgrn(p3, lb, norm_g, LC, LL):
    b, s, _ = p3.shape
    dk = HG_KEY_DIM
    n = HG_HEADS * dk

    def col(name, extra=0):
        base = _DST[name] // dk + extra
        return pl.BlockSpec((None, s, dk), lambda i, h: (i, 0, base + h))

    return pl.pallas_call(
        functools.partial(_hgrn_kernel, LC=LC, LL=LL), grid=(b, HG_HEADS),
        in_specs=[col('hg_q'), col('hg_f'), col('hg_f', HG_HEADS), col('hg_i'), col('hg_g'),
                  pl.BlockSpec((2, dk), lambda i, h: (0, h)),
                  pl.BlockSpec((1, dk), lambda i, h: (0, h))],
        out_specs=pl.BlockSpec((None, s, dk), lambda i, h: (i, 0, h)),
        out_shape=jax.ShapeDtypeStruct((b, s, n), bf16),
        scratch_shapes=[pltpu.VMEM((s, dk), f32), pltpu.VMEM((HG_VAL_DIM, dk), f32)],
        compiler_params=_cp("parallel", "parallel"), name="hgrn",
    )(p3, p3, p3, p3, p3, lb, norm_g.reshape(1, n))


def _mla_prep_kernel(cq_ref, ckv_ref, misc_ref, qg_ref, kvg_ref, wq_ref, wkv_ref, cos_ref, sin_ref,
                     q_out, k_out, v_out):
    h = MLA_HEADS
    scale = (MLA_NOPE + MLA_ROPE) ** -0.5
    cos, sin = cos_ref[...], sin_ref[...]
    qf = _dot(_rms(cq_ref[...].astype(f32), qg_ref[...]).astype(bf16), wq_ref[...]) * scale
    kv = _dot(_rms(ckv_ref[...].astype(f32), kvg_ref[...]).astype(bf16), wkv_ref[...])
    kr = _rope(misc_ref[...].astype(f32), cos, sin)[:, 0:MLA_ROPE].astype(bf16)
    ro = h * MLA_NOPE
    for pp in range(h * MLA_ROPE // LANE):
        qr = _rope(qf[:, ro + pp * LANE:ro + (pp + 1) * LANE], cos, sin).astype(bf16)
        for e in range(LANE // MLA_ROPE):
            q_out[pp * (LANE // MLA_ROPE) + e, :, MLA_NOPE:MLA_NOPE + MLA_ROPE] = qr[:, e * MLA_ROPE:(e + 1) * MLA_ROPE]
    for hh in range(h):
        q_out[hh, :, 0:MLA_NOPE] = qf[:, hh * MLA_NOPE:(hh + 1) * MLA_NOPE].astype(bf16)
        k_out[hh, :, 0:MLA_NOPE] = kv[:, hh * MLA_NOPE:(hh + 1) * MLA_NOPE].astype(bf16)
        k_out[hh, :, MLA_NOPE:MLA_NOPE + MLA_ROPE] = kr
        v_out[hh] = kv[:, ro + hh * MLA_V:ro + (hh + 1) * MLA_V].astype(bf16)


def _mla_prep(p, q_norm_g, kv_norm_g, w_uq_p, w_ukv_p, cos_s, sin_s, S):
    t = p.shape[0]
    tm = ROW_TILE
    ns = S // tm
    h, qd = MLA_HEADS, MLA_NOPE + MLA_ROPE
    return pl.pallas_call(
        _mla_prep_kernel, grid=(t // tm,),
        in_specs=[pl.BlockSpec((tm, MLA_Q_RANK), lambda i: (i, _DST['mla_cq'] // MLA_Q_RANK)),
                  pl.BlockSpec((tm, MLA_KV_RANK), lambda i: (i, _DST['mla_ckv'] // MLA_KV_RANK)),
                  pl.BlockSpec((tm, LANE), lambda i: (i, _MISC // LANE)),
                  pl.BlockSpec((1, MLA_Q_RANK), lambda i: (0, 0)),
                  pl.BlockSpec((1, MLA_KV_RANK), lambda i: (0, 0)),
                  pl.BlockSpec(w_uq_p.shape, lambda i: (0, 0)),
                  pl.BlockSpec(w_ukv_p.shape, lambda i: (0, 0)),
                  pl.BlockSpec((tm, LANE), lambda i: (i % ns, 0)),
                  pl.BlockSpec((tm, LANE), lambda i: (i % ns, 0))],
        out_specs=[pl.BlockSpec((h, tm, qd), lambda i: (0, i, 0)),
                   pl.BlockSpec((h, tm, qd), lambda i: (0, i, 0)),
                   pl.BlockSpec((h, tm, MLA_V), lambda i: (0, i, 0))],
        out_shape=[jax.ShapeDtypeStruct((h, t, qd), bf16), jax.ShapeDtypeStruct((h, t, qd), bf16),
                   jax.ShapeDtypeStruct((h, t, MLA_V), bf16)],
        compiler_params=_cp("parallel"), name="mla_prep",
    )(p, p, p, q_norm_g.reshape(1, -1), kv_norm_g.reshape(1, -1), w_uq_p, w_ukv_p, cos_s, sin_s)


def _mla_attn_kernel(q_ref, k_ref, v_ref, o_ref, *, LC, j0, tq):
    jb = pl.program_id(2) + j0
    s = _dot_nt(q_ref[...], k_ref[...])
    key = lax.broadcasted_iota(jnp.int32, (1, s.shape[1]), 1)
    s = jnp.where((jb < LC // tq) & (key >= LC), -1e30, s)
    e = jnp.exp(s - jnp.max(s, axis=1, keepdims=True))
    o = _dot(e.astype(bf16), v_ref[...]) / jnp.sum(e, axis=1, keepdims=True)
    o_ref[...] = o.astype(o_ref.dtype)


def _mla_attn(qh, kh, vh, B, LC, LL, lat_only):
    h, t, qd = qh.shape
    s = LC + LL
    tq = ROW_TILE
    j0 = LC // tq if lat_only else 0
    nq = s // tq - j0
    ns = s // tq
    return pl.pallas_call(
        functools.partial(_mla_attn_kernel, LC=LC, j0=j0, tq=tq), grid=(B, h, nq),
        in_specs=[pl.BlockSpec((None, tq, qd), lambda b, hh, j: (hh, b * ns + j + j0, 0)),
                  pl.BlockSpec((None, s, qd), lambda b, hh, j: (hh, b, 0)),
                  pl.BlockSpec((None, s, MLA_V), lambda b, hh, j: (hh, b, 0))],
        out_specs=pl.BlockSpec((tq, MLA_V), lambda b, hh, j: (b * ns + j + j0, hh)),
        out_shape=jax.ShapeDtypeStruct((t, h * MLA_V), bf16),
        compiler_params=_cp("parallel", "parallel", "parallel"), name="mla_attn",
    )(qh, kh, vh)


def _merge_kernel(y0, y1, y2, y3, g0, g1, g2, g3, w_ref, o_ref):
    acc = None
    for n, (y_ref, g_ref) in enumerate(((y0, g0), (y1, g1), (y2, g2), (y3, g3))):
        t = _sigmoid(g_ref[...].astype(f32)) * _dot(y_ref[...], w_ref[n])
        acc = t if acc is None else acc + t
    o_ref[...] = acc.astype(o_ref.dtype)


def _merge(ys, p, w_branch_b, rows):
    t = p.shape[0]
    d = D_MODEL
    tm, tn = rows.tm, 512
    gb = _DST['gates'] // tn
    y_specs = [pl.BlockSpec((tm, BRANCH_W), lambda i, j: (rows.src(i), 0)) for _ in range(N_BRANCH)]
    g_specs = [pl.BlockSpec((tm, tn), functools.partial(lambda i, j, n: (rows.src(i), gb + n * (d // tn) + j), n=n))
               for n in range(N_BRANCH)]
    return pl.pallas_call(
        _merge_kernel, grid=(rows.n, d // tn),
        in_specs=y_specs + g_specs + [pl.BlockSpec((N_BRANCH, BRANCH_W, tn), lambda i, j: (0, 0, j))],
        out_specs=pl.BlockSpec((tm, tn), lambda i, j: (i, j)),
        out_shape=jax.ShapeDtypeStruct((rows.n * tm, d), bf16),
        compiler_params=_cp("parallel", "parallel"), name="merge",
    )(*ys, p, p, p, p, w_branch_b)


def _out_proj_kernel(m_ref, w_ref, x_ref, mod_ref, gpost_ref, gffn_ref, rw_ref, x_out, h_out, l_out):
    r = _rms(_dot(m_ref[...], w_ref[...]), gpost_ref[...])
    x = x_ref[...] + mod_ref[2:3, :] * r
    x_out[...] = x
    h2 = _rms(x, gffn_ref[...]) * (1.0 + mod_ref[4:5, :]) + mod_ref[3:4, :]
    h_out[...] = h2.astype(h_out.dtype)
    l_out[...] = _dot_exact(h2, rw_ref[...])


def _out_proj(m, w_out_b, x, mods, g_post, g_ffn, router_w, rows):
    d = D_MODEL
    tm = rows.tm
    n = rows.n * tm
    ne = router_w.shape[1]
    return pl.pallas_call(
        _out_proj_kernel, grid=(rows.n,),
        in_specs=[pl.BlockSpec((tm, d), lambda i: (i, 0)),
                  pl.BlockSpec((d, d), lambda i: (0, 0)),
                  pl.BlockSpec((tm, d), lambda i: (rows.src(i), 0)),
                  pl.BlockSpec((None, 6, d), lambda i: (rows.cond(i), 0, 0)),
                  pl.BlockSpec((1, d), lambda i: (0, 0)),
                  pl.BlockSpec((1, d), lambda i: (0, 0)),
                  pl.BlockSpec((d, ne), lambda i: (0, 0))],
        out_specs=[pl.BlockSpec((tm, d), lambda i: (i, 0)),
                   pl.BlockSpec((tm, d), lambda i: (i, 0)),
                   pl.BlockSpec((tm, ne), lambda i: (i, 0))],
        out_shape=[jax.ShapeDtypeStruct((n, d), f32), jax.ShapeDtypeStruct((2 * n, d), bf16),
                   jax.ShapeDtypeStruct((n, ne), f32)],
        compiler_params=_cp("parallel"), name="out_proj",
    )(m, w_out_b, x, mods, g_post.reshape(1, d), g_ffn.reshape(1, d), router_w)


def _expert_kernel(be_ref, nu_ref, x_ref, wg_ref, wu_ref, wd_ref, o_ref, wg_b, wu_b, wd_b):
    i = pl.program_id(0)

    @pl.when((i == 0) | (be_ref[i] != be_ref[jnp.maximum(i - 1, 0)]))
    def _():
        wg_b[...] = wg_ref[...].astype(bf16)
        wu_b[...] = wu_ref[...].astype(bf16)
        wd_b[...] = wd_ref[...].astype(bf16)

    @pl.when(i < nu_ref[0])
    def _():
        x = x_ref[...]
        hmid = _silu(_dot(x, wg_b[...])) * _dot(x, wu_b[...])
        o_ref[...] = _dot(hmid.astype(bf16), wd_b[...]).astype(o_ref.dtype)

    @pl.when(i >= nu_ref[0])
    def _():
        o_ref[...] = jnp.zeros_like(o_ref)


def _experts(x_sorted, blk_e, n_used, wg, wu, wd, layer):
    n_rows, d = x_sorted.shape
    ff = wg.shape[3]
    bm = MOE_ROWS
    grid_spec = pltpu.PrefetchScalarGridSpec(
        num_scalar_prefetch=2, grid=(n_rows // bm,),
        in_specs=[pl.BlockSpec((bm, d), lambda i, be, nu: (i, 0)),
                  pl.BlockSpec((None, None, d, ff), lambda i, be, nu: (layer, be[i], 0, 0)),
                  pl.BlockSpec((None, None, d, ff), lambda i, be, nu: (layer, be[i], 0, 0)),
                  pl.BlockSpec((None, None, ff, d), lambda i, be, nu: (layer, be[i], 0, 0))],
        out_specs=pl.BlockSpec((bm, d), lambda i, be, nu: (i, 0)),
        scratch_shapes=[pltpu.VMEM((d, ff), bf16), pltpu.VMEM((d, ff), bf16), pltpu.VMEM((ff, d), bf16)])
    return pl.pallas_call(
        _expert_kernel, grid_spec=grid_spec,
        out_shape=jax.ShapeDtypeStruct((n_rows, d), bf16),
        compiler_params=_cp("arbitrary"), name="experts",
    )(blk_e, n_used, x_sorted, wg, wu, wd)


def _ffn_out_kernel(h_ref, r_ref, tw_ref, x_ref, mod_ref, g_ref, sg_ref, su_ref, sd_ref, o_ref):
    h = h_ref[...]
    mid = _silu(_dot(h, sg_ref[...])) * _dot(h, su_ref[...])
    f = _dot(mid.astype(bf16), sd_ref[...])
    for k in range(TOP_K):
        f = f + tw_ref[:, k:k + 1] * r_ref[k].astype(f32)
    o_ref[...] = x_ref[...] + mod_ref[5:6, :] * _rms(f, g_ref[...])


def _ffn_out(h2, routed, top_w, x, mods, g_post, sg, su, sd, rows):
    _, n, d = routed.shape
    tm = rows.tm
    ff = sg.shape[1]
    return pl.pallas_call(
        _ffn_out_kernel, grid=(rows.n,),
        in_specs=[pl.BlockSpec((tm, d), lambda i: (i, 0)),
                  pl.BlockSpec((TOP_K, tm, d), lambda i: (0, i, 0)),
                  pl.BlockSpec((tm, LANE), lambda i: (i, 0)),
                  pl.BlockSpec((tm, d), lambda i: (i, 0)),
                  pl.BlockSpec((None, 6, d), lambda i: (rows.cond(i), 0, 0)),
                  pl.BlockSpec((1, d), lambda i: (0, 0)),
                  pl.BlockSpec((d, ff), lambda i: (0, 0)),
                  pl.BlockSpec((d, ff), lambda i: (0, 0)),
                  pl.BlockSpec((ff, d), lambda i: (0, 0))],
        out_specs=pl.BlockSpec((tm, d), lambda i: (i, 0)),
        out_shape=jax.ShapeDtypeStruct((n, d), f32),
        compiler_params=_cp("parallel"), name="ffn_out",
    )(h2, routed, top_w, x, mods, g_post.reshape(1, d), sg, su, sd)


def _route_kernel(l_ref, b_ref, e_out, w_out):
    scores = _sigmoid(l_ref[...])
    sel = scores + b_ref[...]
    tm, ne = sel.shape
    per_group = ne // N_EXPERT_GROUPS
    lane = lax.broadcasted_iota(jnp.int32, (tm, ne), 1)
    lane_f = lane.astype(f32)
    grp = lane // per_group
    neg = -jnp.inf

    def first_max(v):
        m = jnp.max(v, axis=1, keepdims=True)
        return m, jnp.min(jnp.where(v == m, lane_f, float(ne)), axis=1, keepdims=True)

    gs = []
    for g in range(N_EXPERT_GROUPS):
        sg = jnp.where(grp == g, sel, neg)
        m1, i1 = first_max(sg)
        gs.append(m1 + jnp.max(jnp.where(lane_f == i1, neg, sg), axis=1, keepdims=True))
    chosen = [jnp.zeros((tm, 1), jnp.bool_)] * N_EXPERT_GROUPS
    for _ in range(TOPK_EXPERT_GROUPS):
        m = functools.reduce(jnp.maximum, gs)
        found = jnp.zeros((tm, 1), jnp.bool_)
        for g in range(N_EXPERT_GROUPS):
            hit = (gs[g] == m) & ~found
            chosen[g] = chosen[g] | hit
            found = found | hit
            gs[g] = jnp.where(hit, neg, gs[g])
    keep = jnp.zeros((tm, ne), jnp.bool_)
    for g in range(N_EXPERT_GROUPS):
        keep = keep | ((grp == g) & chosen[g])
    cur = jnp.where(keep, sel, neg)
    ids, wts = [], []
    for _ in range(TOP_K):
        _, idx = first_max(cur)
        hit = lane_f == idx
        ids.append(idx)
        wts.append(jnp.sum(jnp.where(hit, scores, 0.0), axis=1, keepdims=True))
        cur = jnp.where(hit, neg, cur)
    norm = ROUTE_SCALE / functools.reduce(lambda a, b: a + b, wts)
    out_lane = lax.broadcasted_iota(jnp.int32, (tm, LANE), 1)
    e = jnp.zeros((tm, LANE), f32)
    w = jnp.zeros((tm, LANE), f32)
    for k in range(TOP_K):
        e = jnp.where(out_lane == k, ids[k], e)
        w = jnp.where(out_lane == k, wts[k] * norm, w)
    e_out[...] = e.astype(jnp.int32)
    w_out[...] = w


def _route(logits, router_bias):
    t, ne = logits.shape
    tm = ROW_TILE
    e, w = pl.pallas_call(
        _route_kernel, grid=(t // tm,),
        in_specs=[pl.BlockSpec((tm, ne), lambda i: (i, 0)), pl.BlockSpec((1, ne), lambda i: (0, 0))],
        out_specs=[pl.BlockSpec((tm, LANE), lambda i: (i, 0)), pl.BlockSpec((tm, LANE), lambda i: (i, 0))],
        out_shape=[jax.ShapeDtypeStruct((t, LANE), jnp.int32), jax.ShapeDtypeStruct((t, LANE), f32)],
        compiler_params=_cp("parallel"), name="route",
    )(logits, router_bias.astype(f32).reshape(1, ne))
    return e[:, :TOP_K], w


def _moe(h2, logits, router_bias, wg, wu, wd, layer):
    t, d = logits.shape[0], h2.shape[1]
    top_e, w_all = _route(logits, router_bias)
    n_assign = t * TOP_K
    flat_e = top_e.reshape(n_assign)
    onehot = (flat_e[:, None] == jnp.arange(N_EXPERTS)[None, :]).astype(jnp.int32)
    csum = jnp.cumsum(onehot, axis=0)
    counts = csum[-1]
    rank = jnp.take_along_axis(csum, flat_e[:, None], axis=1)[:, 0] - 1
    padded = (counts + MOE_ROWS - 1) // MOE_ROWS * MOE_ROWS
    pad_end = jnp.cumsum(padded)
    dest = (pad_end - padded)[flat_e] + rank
    n_blocks = -(-n_assign // MOE_ROWS) + N_EXPERTS
    n_rows = n_blocks * MOE_ROWS
    row_tok = (jnp.arange(n_rows, dtype=jnp.int32) % t).at[dest].set(jnp.arange(n_assign, dtype=jnp.int32) // TOP_K)
    blk_e = jnp.minimum(jnp.searchsorted(pad_end, jnp.arange(n_blocks) * MOE_ROWS, side='right'),
                        N_EXPERTS - 1).astype(jnp.int32)
    n_used = (pad_end[-1] // MOE_ROWS).astype(jnp.int32).reshape(1)
    x_sorted = h2.at[row_tok].get(mode="promise_in_bounds")
    y_sorted = _experts(x_sorted, blk_e, n_used, wg, wu, wd, layer)
    dest_k = dest.reshape(t, TOP_K).T.reshape(n_assign)
    return y_sorted.at[dest_k].get(mode="promise_in_bounds").reshape(TOP_K, t, d), w_all


def _permute_w_in(w_in):
    cols = [w_in[:, _SRC[name][0]:_SRC[name][0] + _SRC[name][1]] for name in _P_ORDER]
    cols.append(jnp.zeros((w_in.shape[0], _P_WIDTH - _P_USED), w_in.dtype))
    return jnp.concatenate(cols, axis=1).astype(bf16)


def _permute_mla(w_uq, w_ukv):
    h = MLA_HEADS
    q = w_uq.reshape(-1, h, MLA_NOPE + MLA_ROPE)
    w_uq_p = jnp.concatenate([q[:, :, :MLA_NOPE].reshape(-1, h * MLA_NOPE), q[:, :, MLA_NOPE:].reshape(-1, h * MLA_ROPE)], 1)
    kv = w_ukv.reshape(-1, h, MLA_NOPE + MLA_V)
    w_ukv_p = jnp.concatenate([kv[:, :, :MLA_NOPE].reshape(-1, h * MLA_NOPE), kv[:, :, MLA_NOPE:].reshape(-1, h * MLA_V)], 1)
    return w_uq_p.astype(bf16), w_ukv_p.astype(bf16)


def _rope_tables(LC, LL):
    pos = jnp.arange(LL)
    n_freq = SWA_HEAD_DIM // 4
    inv_freq = ROPE_BASE ** (-jnp.arange(n_freq, dtype=f32) / n_freq)
    ang = jnp.concatenate([(pos // GRID_W).astype(f32)[:, None] * inv_freq,
                           (pos % GRID_W).astype(f32)[:, None] * inv_freq], axis=-1)
    cos, sin = jnp.cos(ang), jnp.sin(ang)
    cos = jnp.concatenate([jnp.ones((LC, cos.shape[1]), f32), cos], axis=0)
    sin = jnp.concatenate([jnp.zeros((LC, sin.shape[1]), f32), sin], axis=0)
    return jnp.tile(jnp.concatenate([cos, cos], 1), (1, 2)), jnp.tile(jnp.concatenate([-sin, sin], 1), (1, 2))


def _dir_lanes(v):
    out = jnp.zeros((2, LANE), f32)
    for d in range(2):
        out = out.at[d, _DT_LANE + d * SSD_HEADS:_DT_LANE + (d + 1) * SSD_HEADS].set(v[d].astype(f32))
    return out


def kernel(x, c, ctx, c_ctx, w_mod, b_mod, norm_pre_mix, norm_post_mix, norm_pre_ffn, norm_post_ffn, w_in, ssd_conv_w, ssd_conv_b, ssd_dt_bias, ssd_a_log, ssd_d, ssd_norm, swa_sink, hg_lb_logits, hg_norm, mla_q_norm, mla_kv_norm, mla_w_uq, mla_w_ukv, w_branch, w_out, router_w, router_bias, expert_w_gate, expert_w_up, expert_w_down, shared_w_gate, shared_w_up, shared_w_down):
    B, LL, d = x.shape
    LC = ctx.shape[1]
    S = LC + LL
    depth = w_mod.shape[0]
    assert LC % ROW_TILE == 0 and LL % ROW_TILE == 0 and LL >= 3 * WINDOW and d == D_MODEL

    n_cond = -(-(B + 1) // 8) * 8
    cond = jnp.zeros((n_cond, d), f32).at[:B].set(c).at[B].set(c_ctx)
    mods_all = _modulation(cond, w_mod, b_mod).reshape(depth, n_cond, 6, d)
    cos_s, sin_s = _rope_tables(LC, LL)
    sm = jax.nn.softmax(hg_lb_logits.astype(f32), axis=0)
    lower_bounds = jnp.cumsum(sm, axis=0) - sm[0]

    xs = jnp.concatenate([ctx, x], axis=1).reshape(B * S, d)
    for l in range(depth):
        last = l == depth - 1
        mods = mods_all[l]
        all_rows = _Rows(B, LC, LL, False)
        out_rows = _Rows(B, LC, LL, last)
        h = _norm_mod(xs, norm_pre_mix[l], mods, all_rows)
        p = _matmul(h, _permute_w_in(w_in[l]), 1024, _P_TN, bf16, "in_proj")
        p3 = p.reshape(B, S, _P_WIDTH)

        xbc3 = _ssd_conv(p3, ssd_conv_w[l], ssd_conv_b[l], LC, LL)
        y_ssd = _ssd_scan(xbc3, p3, _dir_lanes(ssd_dt_bias[l]), _dir_lanes(-jnp.exp(ssd_a_log[l].astype(f32))), LC, LL)
        d_in = SSD_HEADS * SSD_HEAD_DIM
        y_ssd = _ssd_finish(y_ssd.reshape(B * S, d_in), xbc3.reshape(B * S, -1), p,
                            jnp.repeat(ssd_d[l].astype(f32), SSD_HEAD_DIM).reshape(1, d_in), ssd_norm[l])
        y_swa = _swa(p3, swa_sink[l].astype(f32), cos_s, sin_s, LC, LL, last).reshape(B * S, -1)
        y_hg = _hgrn(p3, lower_bounds[l], hg_norm[l], LC, LL).reshape(B * S, -1)
        w_uq_p, w_ukv_p = _permute_mla(mla_w_uq[l], mla_w_ukv[l])
        qh, kh, vh = _mla_prep(p, mla_q_norm[l], mla_kv_norm[l], w_uq_p, w_ukv_p, cos_s, sin_s, S)
        y_mla = _mla_attn(qh, kh, vh, B, LC, LL, last)

        m = _merge((y_ssd, y_swa, y_hg, y_mla), p, w_branch[l].astype(bf16), out_rows)
        x_mid, h2, logits = _out_proj(m, w_out[l].astype(bf16), xs, mods, norm_post_mix[l], norm_pre_ffn[l],
                                      router_w[l], out_rows)
        routed, top_w = _moe(h2, logits, router_bias[l], expert_w_gate, expert_w_up, expert_w_down, l)
        xs = _ffn_out(h2, routed, top_w, x_mid, mods, norm_post_ffn[l], shared_w_gate[l].astype(bf16),
                      shared_w_up[l].astype(bf16), shared_w_down[l].astype(bf16), _Rows(B, LC, LL, last, tm=LANE))
    return xs.reshape(B, LL, d)
```

```python
import functools
import math

import jax
import jax.numpy as jnp
import numpy as np
from jax import lax
from jax.experimental import pallas as pl
from jax.experimental.pallas import tpu as pltpu

f32 = jnp.float32
bf16 = jnp.bfloat16

D_MODEL = 2048
GRID_W = 64
N_BRANCH = 4
BRANCH_W = 1024
EPS = 1e-6
ROPE_BASE = 10000.0

SSD_HEADS = 16
SSD_HEAD_DIM = 64
SSD_GROUPS = 4
SSD_STATE = 128
SSD_CONV = 5
SSD_CHUNK = 128

SWA_HEADS = 16
SWA_KV_HEADS = 4
SWA_HEAD_DIM = 64
WINDOW = 128

HG_HEADS = 8
HG_KEY_DIM = 128
HG_VAL_DIM = 128
HG_CHUNK = 64
HG_DIAG = 8

MLA_HEADS = 8
MLA_Q_RANK = 512
MLA_KV_RANK = 256
MLA_NOPE = 128
MLA_ROPE = 64
MLA_V = 128

N_EXPERTS = 64
TOP_K = 6
N_EXPERT_GROUPS = 8
TOPK_EXPERT_GROUPS = 4
EXPERT_FF = 512
SHARED_FF = 1024
ROUTE_SCALE = 2.5
MOE_ROWS = 256

LANE = 128
ROW_TILE = 256
VMEM_LIMIT = 48 * 1024 * 1024

_IN_LAYOUT = (
    ('ssd_z', 1024), ('ssd_x', 1024), ('ssd_B', 512), ('ssd_C', 512), ('ssd_dt', 32),
    ('swa_q', 1024), ('swa_k', 256), ('swa_v', 256),
    ('hg_q', 1024), ('hg_f', 2048), ('hg_i', 1024), ('hg_g', 1024),
    ('mla_cq', 512), ('mla_ckv', 256), ('mla_kr', 64), ('gates', 8192),
)
_P_ORDER = ('gates', 'hg_f', 'ssd_x', 'ssd_B', 'ssd_C', 'ssd_z', 'swa_q', 'hg_q', 'hg_i', 'hg_g',
            'mla_cq', 'swa_k', 'swa_v', 'mla_ckv', 'mla_kr', 'ssd_dt')
_P_TN = 512


def _layout():
    src, off = {}, 0
    for name, size in _IN_LAYOUT:
        src[name] = (off, size)
        off += size
    dst, off = {}, 0
    for name in _P_ORDER:
        dst[name] = off
        off += src[name][1]
    width = -(-off // _P_TN) * _P_TN
    return src, dst, off, width


_SRC, _DST, _P_USED, _P_WIDTH = _layout()
_MISC = _DST['mla_kr']
_DT_LANE = _DST['ssd_dt'] - _MISC


def _cp(*sem):
    return pltpu.CompilerParams(dimension_semantics=sem, vmem_limit_bytes=VMEM_LIMIT)


def _sigmoid(x):
    return 1.0 / (1.0 + jnp.exp(-x))


def _silu(x):
    return x * _sigmoid(x)


def _rms(x, gain):
    return x * lax.rsqrt(jnp.mean(x * x, axis=-1, keepdims=True) + EPS) * gain


def _dot(a, b):
    return jnp.dot(a, b, preferred_element_type=f32)


def _dot_nt(a, b):
    return lax.dot_general(a, b, (((1,), (1,)), ((), ())), preferred_element_type=f32)


def _dot_tn(a, b):
    return lax.dot_general(a, b, (((0,), (0,)), ((), ())), preferred_element_type=f32)


def _dot_exact(a, b):
    return jnp.dot(a, b, preferred_element_type=f32, precision=lax.Precision.HIGHEST)


class _Rows:
    def __init__(self, B, LC, LL, lat_only, tm=ROW_TILE):
        self.B, self.LC, self.LL, self.tm = B, LC, LL, tm
        self.nC, self.nL = LC // tm, LL // tm
        self.nS = self.nC + self.nL
        self.lat_only = lat_only
        self.n = B * (self.nL if lat_only else self.nS)

    def src(self, i):
        if self.lat_only:
            return (i // self.nL) * self.nS + self.nC + i % self.nL
        return i

    def cond(self, i):
        if self.lat_only:
            return i // self.nL
        return jnp.where(i % self.nS < self.nC, self.B, i // self.nS)

    def pos(self, i):
        if self.lat_only:
            return self.nC + i % self.nL
        return i % self.nS


def _mod_kernel(c_ref, w_ref, b_ref, o_ref):
    s = _silu(c_ref[...]).astype(bf16)
    o_ref[...] = _dot(s, w_ref[...].astype(bf16)) + b_ref[...]


def _modulation(cond, w_mod, b_mod):
    nl, d, n = w_mod.shape
    nc = cond.shape[0]
    tn = 1024
    return pl.pallas_call(
        _mod_kernel, grid=(nl, n // tn),
        in_specs=[pl.BlockSpec((nc, d), lambda l, j: (0, 0)),
                  pl.BlockSpec((None, d, tn), lambda l, j: (l, 0, j)),
                  pl.BlockSpec((None, 1, tn), lambda l, j: (l, 0, j))],
        out_specs=pl.BlockSpec((None, nc, tn), lambda l, j: (l, 0, j)),
        out_shape=jax.ShapeDtypeStruct((nl, nc, n), f32),
        compiler_params=_cp("parallel", "parallel"), name="modulation",
    )(cond, w_mod, b_mod.reshape(nl, 1, n))


def _norm_mod_kernel(x_ref, g_ref, mod_ref, o_ref):
    y = _rms(x_ref[...], g_ref[...])
    o_ref[...] = (y * (1.0 + mod_ref[1:2, :]) + mod_ref[0:1, :]).astype(o_ref.dtype)


def _norm_mod(x, gain, mods, rows):
    t, d = x.shape
    tm = rows.tm
    return pl.pallas_call(
        _norm_mod_kernel, grid=(rows.n,),
        in_specs=[pl.BlockSpec((tm, d), lambda i: (rows.src(i), 0)),
                  pl.BlockSpec((1, d), lambda i: (0, 0)),
                  pl.BlockSpec((None, 6, d), lambda i: (rows.cond(i), 0, 0))],
        out_specs=pl.BlockSpec((tm, d), lambda i: (rows.src(i), 0)),
        out_shape=jax.ShapeDtypeStruct((t, d), bf16),
        compiler_params=_cp("parallel"), name="norm_mod",
    )(x, gain.reshape(1, d), mods)


def _mm_kernel(a_ref, b_ref, o_ref):
    o_ref[...] = _dot(a_ref[...], b_ref[...]).astype(o_ref.dtype)


def _matmul(a, b, tm, tn, out_dtype, name):
    m, k = a.shape
    n = b.shape[1]
    return pl.pallas_call(
        _mm_kernel, grid=(m // tm, n // tn),
        in_specs=[pl.BlockSpec((tm, k), lambda i, j: (i, 0)),
                  pl.BlockSpec((k, tn), lambda i, j: (0, j))],
        out_specs=pl.BlockSpec((tm, tn), lambda i, j: (i, j)),
        out_shape=jax.ShapeDtypeStruct((m, n), out_dtype),
        compiler_params=_cp("parallel", "parallel"), name=name,
    )(a, b)


def _ssd_conv_kernel(x_ref, w_ref, b_ref, o_ref, *, LC, LL):
    s = LC + LL
    x = x_ref[...].astype(f32)
    row = lax.broadcasted_iota(jnp.int32, (s, 1), 0)
    pos = jnp.where(row < LC, row, row - LC)
    length = jnp.where(row < LC, LC, LL)
    acc = jnp.zeros_like(x) + b_ref[...]
    half = SSD_CONV // 2
    for k in range(SSD_CONV):
        sh = k - half
        xs = x if sh == 0 else pltpu.roll(x, (-sh) % s, 0)
        valid = (pos + sh >= 0) & (pos + sh < length)
        acc = acc + w_ref[k:k + 1, :] * jnp.where(valid, xs, 0.0)
    o_ref[...] = _silu(acc).astype(o_ref.dtype)


def _ssd_conv(p3, conv_w, conv_b, LC, LL):
    b, s, _ = p3.shape
    n = conv_w.shape[1]
    tc = 256
    base = _DST['ssd_x'] // tc
    return pl.pallas_call(
        functools.partial(_ssd_conv_kernel, LC=LC, LL=LL), grid=(b, n // tc),
        in_specs=[pl.BlockSpec((None, s, tc), lambda i, j: (i, 0, base + j)),
                  pl.BlockSpec((SSD_CONV, tc), lambda i, j: (0, j)),
                  pl.BlockSpec((1, tc), lambda i, j: (0, j))],
        out_specs=pl.BlockSpec((None, s, tc), lambda i, j: (i, 0, j)),
        out_shape=jax.ShapeDtypeStruct((b, s, n), bf16),
        compiler_params=_cp("parallel", "parallel"), name="ssd_conv",
    )(p3, conv_w, conv_b.reshape(1, n))


def _chunk_start(c, n_ctx, n_all, q, rev):
    if rev:
        c = jnp.where(c < n_ctx, n_ctx - 1 - c, n_all - 1 - (c - n_ctx))
    return pl.multiple_of(c * q, q)


def _pick4(hid, vals):
    return jnp.where(hid == 0, vals[0], jnp.where(hid == 1, vals[1], jnp.where(hid == 2, vals[2], vals[3])))


def _ssd_scan_kernel(x_ref, b_ref, c_ref, dt_ref, bias_ref, a_ref, o_ref, st_ref, *, LC, LL):
    q = SSD_CHUNK
    r = SSD_HEADS // SSD_GROUPS
    g = pl.program_id(1)
    n_ctx, n_all = LC // q, (LC + LL) // q
    ii = lax.broadcasted_iota(jnp.int32, (q, q), 0)
    jj = lax.broadcasted_iota(jnp.int32, (q, q), 1)
    lane = lax.broadcasted_iota(jnp.int32, (1, LANE), 1)
    sub = lax.broadcasted_iota(jnp.int32, (LANE, 1), 0)
    hid = lax.broadcasted_iota(jnp.int32, (1, r * SSD_HEAD_DIM), 1) // SSD_HEAD_DIM
    lane_lo = lane < SSD_HEAD_DIM

    for d in range(2):
        rev = d == 1
        tri = (ii <= jj) if rev else (ii >= jj)
        tri_f = tri.astype(f32)
        last = 0 if rev else q - 1
        st_ref[...] = jnp.zeros_like(st_ref)

        def chunk(c, carry, d=d, rev=rev, tri=tri, tri_f=tri_f, last=last):
            s = _chunk_start(c, n_ctx, n_all, q, rev)
            raw = dt_ref[pl.ds(s, q), :].astype(f32) + bias_ref[d:d + 1, :]
            dtv = jnp.maximum(raw, 0.0) + jnp.log1p(jnp.exp(-jnp.abs(raw)))
            a = dtv * a_ref[d:d + 1, :]
            acum = _dot_exact(tri_f, a)
            acum_t = acum.T
            x = x_ref[pl.ds(s, q), :].astype(f32)
            bm = b_ref[pl.ds(s, q), :]
            cm = c_ref[pl.ds(s, q), :]
            cb = _dot_nt(cm, bm)
            acol, arow, dcol, alast = [], [], [], []
            for hh in range(r):
                col = _DT_LANE + d * SSD_HEADS + g * r + hh
                acol.append(jnp.sum(jnp.where(lane == col, acum, 0.0), axis=1, keepdims=True))
                arow.append(jnp.sum(jnp.where(sub == col, acum_t, 0.0), axis=0, keepdims=True))
                dcol.append(jnp.sum(jnp.where(lane == col, dtv, 0.0), axis=1, keepdims=True))
                alast.append(acol[hh][last:last + 1, :])
            a_all = _pick4(hid, acol)
            al_all = _pick4(hid, alast)
            dtx = _pick4(hid, dcol) * x
            st = st_ref[...]
            y_off = _dot(cm, st.astype(bf16)) * jnp.exp(a_all)
            y_diag = []
            for pp in range(r // 2):
                decay = [jnp.exp(jnp.where(tri, acol[h] - arow[h], -1e30)) * cb for h in (2 * pp, 2 * pp + 1)]
                lhs = jnp.concatenate(decay, axis=1).astype(bf16)
                xp = dtx[:, pp * LANE:(pp + 1) * LANE]
                rhs = jnp.concatenate([jnp.where(lane_lo, xp, 0.0), jnp.where(lane_lo, 0.0, xp)], axis=0)
                y_diag.append(_dot(lhs, rhs.astype(bf16)))
            y = y_off + jnp.concatenate(y_diag, axis=1)
            if rev:
                o_ref[pl.ds(s, q), :] += y
            else:
                o_ref[pl.ds(s, q), :] = y
            w = (jnp.exp(al_all - a_all) * dtx).astype(bf16)
            st_ref[...] = st * jnp.exp(al_all) + _dot_tn(bm, w)
            return carry

        lax.fori_loop(0, n_all, chunk, 0, unroll=3)


def _ssd_scan(xbc3, p3, bias_v, a_v, LC, LL):
    b, s, _ = xbc3.shape
    d_in = SSD_HEADS * SSD_HEAD_DIM
    gw = d_in // SSD_GROUPS
    nb = d_in // SSD_STATE
    return pl.pallas_call(
        functools.partial(_ssd_scan_kernel, LC=LC, LL=LL), grid=(b, SSD_GROUPS),
        in_specs=[pl.BlockSpec((None, s, gw), lambda i, g: (i, 0, g)),
                  pl.BlockSpec((None, s, SSD_STATE), lambda i, g: (i, 0, nb + g)),
                  pl.BlockSpec((None, s, SSD_STATE), lambda i, g: (i, 0, nb + SSD_GROUPS + g)),
                  pl.BlockSpec((None, s, LANE), lambda i, g: (i, 0, _MISC // LANE)),
                  pl.BlockSpec((2, LANE), lambda i, g: (0, 0)),
                  pl.BlockSpec((2, LANE), lambda i, g: (0, 0))],
        out_specs=pl.BlockSpec((None, s, gw), lambda i, g: (i, 0, g)),
        out_shape=jax.ShapeDtypeStruct((b, s, d_in), f32),
        scratch_shapes=[pltpu.VMEM((SSD_STATE, gw), f32)],
        compiler_params=_cp("parallel", "parallel"), name="ssd_scan",
    )(xbc3, xbc3, xbc3, p3, bias_v, a_v)


def _ssd_finish_kernel(y_ref, x_ref, z_ref, dsk_ref, g_ref, o_ref):
    y = y_ref[...] + x_ref[...].astype(f32) * dsk_ref[...]
    o_ref[...] = _rms(y * _silu(z_ref[...].astype(f32)), g_ref[...]).astype(o_ref.dtype)


def _ssd_finish(y, xbc, p, d_skip_row, norm_g):
    t, n = y.shape
    tm = ROW_TILE
    zb = _DST['ssd_z'] // n
    return pl.pallas_call(
        _ssd_finish_kernel, grid=(t // tm,),
        in_specs=[pl.BlockSpec((tm, n), lambda i: (i, 0)),
                  pl.BlockSpec((tm, n), lambda i: (i, 0)),
                  pl.BlockSpec((tm, n), lambda i: (i, zb)),
                  pl.BlockSpec((1, n), lambda i: (0, 0)),
                  pl.BlockSpec((1, n), lambda i: (0, 0))],
        out_specs=pl.BlockSpec((tm, n), lambda i: (i, 0)),
        out_shape=jax.ShapeDtypeStruct((t, n), bf16),
        compiler_params=_cp("parallel"), name="ssd_finish",
    )(y, xbc, p, d_skip_row, norm_g.reshape(1, n))


def _rope(x, cos, sin):
    lane = lax.broadcasted_iota(jnp.int32, (1, LANE), 1)
    first = (lane % 64) < 32
    partner = jnp.where(first, pltpu.roll(x, LANE - 32, 1), pltpu.roll(x, 32, 1))
    return x * cos + partner * sin


def _swa_kernel(sink_ref, q_ref, k_ref, v_ref, cos_ref, sin_ref, o_ref, *, LC, LL, j0):
    w = WINDOW
    hd = SWA_HEAD_DIM
    grp = SWA_HEADS // SWA_KV_HEADS
    kv_w = SWA_KV_HEADS * hd
    jb = pl.program_id(1) + j0
    n_ctx = LC // w
    is_lat = jb >= n_ctx
    nb = jnp.maximum(jb - n_ctx, 0)
    qs = pl.multiple_of(jb * w, w)
    k0 = pl.multiple_of(jnp.clip((nb - 1) * w, 0, LL - 3 * w), w)
    ks = pl.multiple_of(LC + k0, w)
    scale = hd ** -0.5

    cos_q, sin_q = cos_ref[pl.ds(qs, w), :], sin_ref[pl.ds(qs, w), :]
    cos_k, sin_k = cos_ref[pl.ds(ks, 3 * w), :], sin_ref[pl.ds(ks, 3 * w), :]
    q = q_ref[...].astype(f32)
    q = jnp.concatenate([_rope(q[:, c * LANE:(c + 1) * LANE], cos_q, sin_q) for c in range(SWA_HEADS * hd // LANE)],
                        axis=1) * scale
    kb = k_ref[pl.ds(ks, 3 * w), :].astype(f32)
    kb = jnp.concatenate([_rope(kb[:, c * LANE:(c + 1) * LANE], cos_k, sin_k) for c in range(kv_w // LANE)],
                         axis=1).astype(bf16)
    vb = v_ref[pl.ds(ks, 3 * w), :]
    kc = k_ref[0:LC, :]
    vc = v_ref[0:LC, :]

    rows = grp * w
    q_abs = nb * w + lax.broadcasted_iota(jnp.int32, (rows, 1), 0) % w
    k_abs = k0 + lax.broadcasted_iota(jnp.int32, (1, 3 * w), 1)
    valid = is_lat & (jnp.abs(k_abs - q_abs) <= w)
    head_of_row = lax.broadcasted_iota(jnp.int32, (rows, 1), 0) // w

    outs = []
    for g in range(SWA_KV_HEADS):
        qg = jnp.concatenate([q[:, (g * grp + r) * hd:(g * grp + r + 1) * hd] for r in range(grp)], axis=0).astype(bf16)
        sl = slice(g * hd, (g + 1) * hd)
        s_lat = jnp.where(valid, _dot_nt(qg, kb[:, sl]), -1e30)
        s_ctx = _dot_nt(qg, kc[:, sl])
        sink = jnp.zeros((rows, 1), f32)
        for r in range(grp):
            sink = jnp.where(head_of_row == r, sink_ref[g * grp + r], sink)
        m = jnp.maximum(jnp.maximum(jnp.max(s_lat, axis=1, keepdims=True), jnp.max(s_ctx, axis=1, keepdims=True)), sink)
        e_lat = jnp.exp(s_lat - m)
        e_ctx = jnp.exp(s_ctx - m)
        den = jnp.sum(e_lat, axis=1, keepdims=True) + jnp.sum(e_ctx, axis=1, keepdims=True) + jnp.exp(sink - m)
        og = (_dot(e_lat.astype(bf16), vb[:, sl]) + _dot(e_ctx.astype(bf16), vc[:, sl])) / den
        outs.extend(og[r * w:(r + 1) * w, :] for r in range(grp))
    o_ref[...] = jnp.concatenate(outs, axis=1).astype(o_ref.dtype)


def _swa(p3, sink, cos_s, sin_s, LC, LL, lat_only):
    b, s, _ = p3.shape
    w = WINDOW
    j0 = LC // w if lat_only else 0
    nq = s // w - j0
    qw = SWA_HEADS * SWA_HEAD_DIM
    kw = SWA_KV_HEADS * SWA_HEAD_DIM
    return pl.pallas_call(
        functools.partial(_swa_kernel, LC=LC, LL=LL, j0=j0), grid=(b, nq),
        in_specs=[pl.BlockSpec(memory_space=pltpu.SMEM),
                  pl.BlockSpec((None, w, qw), lambda i, j: (i, j + j0, _DST['swa_q'] // qw)),
                  pl.BlockSpec((None, s, kw), lambda i, j: (i, 0, _DST['swa_k'] // kw)),
                  pl.BlockSpec((None, s, kw), lambda i, j: (i, 0, _DST['swa_v'] // kw)),
                  pl.BlockSpec((s, LANE), lambda i, j: (0, 0)),
                  pl.BlockSpec((s, LANE), lambda i, j: (0, 0))],
        out_specs=pl.BlockSpec((None, w, qw), lambda i, j: (i, j + j0, 0)),
        out_shape=jax.ShapeDtypeStruct((b, s, qw), bf16),
        compiler_params=_cp("parallel", "parallel"), name="swa",
    )(sink, p3, p3, p3, cos_s, sin_s)


def _hgrn_kernel(q_ref, ff_ref, fr_ref, v_ref, g_ref, lb_ref, ng_ref, o_ref, acc_ref, st_ref, *, LC, LL):
    q = HG_CHUNK
    n_ctx, n_all = LC // q, (LC + LL) // q
    ii = lax.broadcasted_iota(jnp.int32, (q, q), 0)
    jj = lax.broadcasted_iota(jnp.int32, (q, q), 1)
    r8 = lax.broadcasted_iota(jnp.int32, (q, 1), 0) % HG_DIAG
    levels = []
    sz = HG_DIAG
    while sz < q:
        levels.append(sz)
        sz *= 2

    for d in range(2):
        rev = d == 1
        tri_f = ((ii <= jj) if rev else (ii >= jj)).astype(f32)
        last = 0 if rev else q - 1
        f_ref = fr_ref if rev else ff_ref
        st_ref[...] = jnp.zeros_like(st_ref)
        masks = []
        for sz in levels:
            same = (ii // (2 * sz)) == (jj // (2 * sz))
            i_hi, j_hi = (ii % (2 * sz)) >= sz, (jj % (2 * sz)) >= sz
            masks.append(same & (~i_hi & j_hi if rev else i_hi & ~j_hi))

        def chunk(c, carry, d=d, rev=rev, tri_f=tri_f, last=last, f_ref=f_ref, masks=masks):
            s = _chunk_start(c, n_ctx, n_all, q, rev)
            lb = lb_ref[d:d + 1, :]
            qq = _silu(q_ref[pl.ds(s, q), :].astype(f32))
            f = lb + (1.0 - lb) * _sigmoid(f_ref[pl.ds(s, q), :].astype(f32))
            kk = 1.0 - f
            v = v_ref[pl.ds(s, q), :].astype(f32)
            cum = _dot_exact(tri_f, jnp.log(f))
            cl = cum[last:last + 1, :]
            st = st_ref[...]
            o = _dot_nt((qq * jnp.exp(cum)).astype(bf16), st.astype(bf16))
            att = jnp.zeros((q, q), f32)
            for sz, mask in zip(levels, masks):
                pieces = []
                for b0 in range(0, q, 2 * sz):
                    rr = b0 + sz if rev else b0 + sz - 1
                    pieces.append(jnp.broadcast_to(cum[rr:rr + 1, :], (2 * sz, cum.shape[1])))
                e = jnp.exp(-jnp.abs(cum - jnp.concatenate(pieces, axis=0)))
                att = att + jnp.where(mask, _dot_nt((qq * e).astype(bf16), (kk * e).astype(bf16)), 0.0)
            o = o + _dot(att.astype(bf16), v.astype(bf16))
            o = o + jnp.sum(qq * kk, axis=1, keepdims=True) * v
            for dd in range(1, HG_DIAG):
                sh = q - dd if rev else dd
                ks, cs, vs = pltpu.roll(kk, sh, 0), pltpu.roll(cum, sh, 0), pltpu.roll(v, sh, 0)
                ok = (r8 <= HG_DIAG - 1 - dd) if rev else (r8 >= dd)
                pr = jnp.where(ok, qq * ks * jnp.exp(cum - cs), 0.0)
                o = o + jnp.sum(pr, axis=1, keepdims=True) * vs
            st_ref[...] = st * jnp.exp(cl) + _dot_tn(v.astype(bf16), (kk * jnp.exp(cl - cum)).astype(bf16))
            if rev:
                y = _rms(acc_ref[pl.ds(s, q), :] + o, ng_ref[...])
                o_ref[pl.ds(s, q), :] = (y * _sigmoid(g_ref[pl.ds(s, q), :].astype(f32))).astype(o_ref.dtype)
            else:
                acc_ref[pl.ds(s, q), :] = o
            return carry

        lax.fori_loop(0, n_all, chunk, 0, unroll=6)


def _hgrn(p3, lb, norm_g, LC, LL):
    b, s, _ = p3.shape
    dk = HG_KEY_DIM
    n = HG_HEADS * dk

    def col(name, extra=0):
        base = _DST[name] // dk + extra
        return pl.BlockSpec((None, s, dk), lambda i, h: (i, 0, base + h))

    return pl.pallas_call(
        functools.partial(_hgrn_kernel, LC=LC, LL=LL), grid=(b, HG_HEADS),
        in_specs=[col('hg_q'), col('hg_f'), col('hg_f', HG_HEADS), col('hg_i'), col('hg_g'),
                  pl.BlockSpec((2, dk), lambda i, h: (0, h)),
                  pl.BlockSpec((1, dk), lambda i, h: (0, h))],
        out_specs=pl.BlockSpec((None, s, dk), lambda i, h: (i, 0, h)),
        out_shape=jax.ShapeDtypeStruct((b, s, n), bf16),
        scratch_shapes=[pltpu.VMEM((s, dk), f32), pltpu.VMEM((HG_VAL_DIM, dk), f32)],
        compiler_params=_cp("parallel", "parallel"), name="hgrn",
    )(p3, p3, p3, p3, p3, lb, norm_g.reshape(1, n))


def _mla_prep_kernel(cq_ref, ckv_ref, misc_ref, qg_ref, kvg_ref, wq_ref, wkv_ref, cos_ref, sin_ref,
                     q_out, k_out, v_out):
    h = MLA_HEADS
    scale = (MLA_NOPE + MLA_ROPE) ** -0.5
    cos, sin = cos_ref[...], sin_ref[...]
    qf = _dot(_rms(cq_ref[...].astype(f32), qg_ref[...]).astype(bf16), wq_ref[...]) * scale
    kv = _dot(_rms(ckv_ref[...].astype(f32), kvg_ref[...]).astype(bf16), wkv_ref[...])
    kr = _rope(misc_ref[...].astype(f32), cos, sin)[:, 0:MLA_ROPE].astype(bf16)
    ro = h * MLA_NOPE
    for pp in range(h * MLA_ROPE // LANE):
        qr = _rope(qf[:, ro + pp * LANE:ro + (pp + 1) * LANE], cos, sin).astype(bf16)
        for e in range(LANE // MLA_ROPE):
            q_out[pp * (LANE // MLA_ROPE) + e, :, MLA_NOPE:MLA_NOPE + MLA_ROPE] = qr[:, e * MLA_ROPE:(e + 1) * MLA_ROPE]
    for hh in range(h):
        q_out[hh, :, 0:MLA_NOPE] = qf[:, hh * MLA_NOPE:(hh + 1) * MLA_NOPE].astype(bf16)
        k_out[hh, :, 0:MLA_NOPE] = kv[:, hh * MLA_NOPE:(hh + 1) * MLA_NOPE].astype(bf16)
        k_out[hh, :, MLA_NOPE:MLA_NOPE + MLA_ROPE] = kr
        v_out[hh] = kv[:, ro + hh * MLA_V:ro + (hh + 1) * MLA_V].astype(bf16)


def _mla_prep(p, q_norm_g, kv_norm_g, w_uq_p, w_ukv_p, cos_s, sin_s, S):
    t = p.shape[0]
    tm = ROW_TILE
    ns = S // tm
    h, qd = MLA_HEADS, MLA_NOPE + MLA_ROPE
    return pl.pallas_call(
        _mla_prep_kernel, grid=(t // tm,),
        in_specs=[pl.BlockSpec((tm, MLA_Q_RANK), lambda i: (i, _DST['mla_cq'] // MLA_Q_RANK)),
                  pl.BlockSpec((tm, MLA_KV_RANK), lambda i: (i, _DST['mla_ckv'] // MLA_KV_RANK)),
                  pl.BlockSpec((tm, LANE), lambda i: (i, _MISC // LANE)),
                  pl.BlockSpec((1, MLA_Q_RANK), lambda i: (0, 0)),
                  pl.BlockSpec((1, MLA_KV_RANK), lambda i: (0, 0)),
                  pl.BlockSpec(w_uq_p.shape, lambda i: (0, 0)),
                  pl.BlockSpec(w_ukv_p.shape, lambda i: (0, 0)),
                  pl.BlockSpec((tm, LANE), lambda i: (i % ns, 0)),
                  pl.BlockSpec((tm, LANE), lambda i: (i % ns, 0))],
        out_specs=[pl.BlockSpec((h, tm, qd), lambda i: (0, i, 0)),
                   pl.BlockSpec((h, tm, qd), lambda i: (0, i, 0)),
                   pl.BlockSpec((h, tm, MLA_V), lambda i: (0, i, 0))],
        out_shape=[jax.ShapeDtypeStruct((h, t, qd), bf16), jax.ShapeDtypeStruct((h, t, qd), bf16),
                   jax.ShapeDtypeStruct((h, t, MLA_V), bf16)],
        compiler_params=_cp("parallel"), name="mla_prep",
    )(p, p, p, q_norm_g.reshape(1, -1), kv_norm_g.reshape(1, -1), w_uq_p, w_ukv_p, cos_s, sin_s)


def _mla_attn_kernel(q_ref, k_ref, v_ref, o_ref, *, LC, j0, tq):
    jb = pl.program_id(2) + j0
    s = _dot_nt(q_ref[...], k_ref[...])
    key = lax.broadcasted_iota(jnp.int32, (1, s.shape[1]), 1)
    s = jnp.where((jb < LC // tq) & (key >= LC), -1e30, s)
    e = jnp.exp(s - jnp.max(s, axis=1, keepdims=True))
    o = _dot(e.astype(bf16), v_ref[...]) / jnp.sum(e, axis=1, keepdims=True)
    o_ref[...] = o.astype(o_ref.dtype)


def _mla_attn(qh, kh, vh, B, LC, LL, lat_only):
    h, t, qd = qh.shape
    s = LC + LL
    tq = ROW_TILE
    j0 = LC // tq if lat_only else 0
    nq = s // tq - j0
    ns = s // tq
    return pl.pallas_call(
        functools.partial(_mla_attn_kernel, LC=LC, j0=j0, tq=tq), grid=(B, h, nq),
        in_specs=[pl.BlockSpec((None, tq, qd), lambda b, hh, j: (hh, b * ns + j + j0, 0)),
                  pl.BlockSpec((None, s, qd), lambda b, hh, j: (hh, b, 0)),
                  pl.BlockSpec((None, s, MLA_V), lambda b, hh, j: (hh, b, 0))],
        out_specs=pl.BlockSpec((tq, MLA_V), lambda b, hh, j: (b * ns + j + j0, hh)),
        out_shape=jax.ShapeDtypeStruct((t, h * MLA_V), bf16),
        compiler_params=_cp("parallel", "parallel", "parallel"), name="mla_attn",
    )(qh, kh, vh)


def _merge_kernel(y0, y1, y2, y3, g0, g1, g2, g3, w_ref, o_ref):
    acc = None
    for n, (y_ref, g_ref) in enumerate(((y0, g0), (y1, g1), (y2, g2), (y3, g3))):
        t = _sigmoid(g_ref[...].astype(f32)) * _dot(y_ref[...], w_ref[n])
        acc = t if acc is None else acc + t
    o_ref[...] = acc.astype(o_ref.dtype)


def _merge(ys, p, w_branch_b, rows):
    t = p.shape[0]
    d = D_MODEL
    tm, tn = rows.tm, 512
    gb = _DST['gates'] // tn
    y_specs = [pl.BlockSpec((tm, BRANCH_W), lambda i, j: (rows.src(i), 0)) for _ in range(N_BRANCH)]
    g_specs = [pl.BlockSpec((tm, tn), functools.partial(lambda i, j, n: (rows.src(i), gb + n * (d // tn) + j), n=n))
               for n in range(N_BRANCH)]
    return pl.pallas_call(
        _merge_kernel, grid=(rows.n, d // tn),
        in_specs=y_specs + g_specs + [pl.BlockSpec((N_BRANCH, BRANCH_W, tn), lambda i, j: (0, 0, j))],
        out_specs=pl.BlockSpec((tm, tn), lambda i, j: (i, j)),
        out_shape=jax.ShapeDtypeStruct((rows.n * tm, d), bf16),
        compiler_params=_cp("parallel", "parallel"), name="merge",
    )(*ys, p, p, p, p, w_branch_b)


def _out_proj_kernel(m_ref, w_ref, x_ref, mod_ref, gpost_ref, gffn_ref, rw_ref, x_out, h_out, l_out):
    r = _rms(_dot(m_ref[...], w_ref[...]), gpost_ref[...])
    x = x_ref[...] + mod_ref[2:3, :] * r
    x_out[...] = x
    h2 = _rms(x, gffn_ref[...]) * (1.0 + mod_ref[4:5, :]) + mod_ref[3:4, :]
    h_out[...] = h2.astype(h_out.dtype)
    l_out[...] = _dot_exact(h2, rw_ref[...])


def _out_proj(m, w_out_b, x, mods, g_post, g_ffn, router_w, rows):
    d = D_MODEL
    tm = rows.tm
    n = rows.n * tm
    ne = router_w.shape[1]
    return pl.pallas_call(
        _out_proj_kernel, grid=(rows.n,),
        in_specs=[pl.BlockSpec((tm, d), lambda i: (i, 0)),
                  pl.BlockSpec((d, d), lambda i: (0, 0)),
                  pl.BlockSpec((tm, d), lambda i: (rows.src(i), 0)),
                  pl.BlockSpec((None, 6, d), lambda i: (rows.cond(i), 0, 0)),
                  pl.BlockSpec((1, d), lambda i: (0, 0)),
                  pl.BlockSpec((1, d), lambda i: (0, 0)),
                  pl.BlockSpec((d, ne), lambda i: (0, 0))],
        out_specs=[pl.BlockSpec((tm, d), lambda i: (i, 0)),
                   pl.BlockSpec((tm, d), lambda i: (i, 0)),
                   pl.BlockSpec((tm, ne), lambda i: (i, 0))],
        out_shape=[jax.ShapeDtypeStruct((n, d), f32), jax.ShapeDtypeStruct((2 * n, d), bf16),
                   jax.ShapeDtypeStruct((n, ne), f32)],
        compiler_params=_cp("parallel"), name="out_proj",
    )(m, w_out_b, x, mods, g_post.reshape(1, d), g_ffn.reshape(1, d), router_w)


def _expert_kernel(be_ref, nu_ref, x_ref, wg_ref, wu_ref, wd_ref, o_ref, wg_b, wu_b, wd_b):
    i = pl.program_id(0)

    @pl.when((i == 0) | (be_ref[i] != be_ref[jnp.maximum(i - 1, 0)]))
    def _():
        wg_b[...] = wg_ref[...].astype(bf16)
        wu_b[...] = wu_ref[...].astype(bf16)
        wd_b[...] = wd_ref[...].astype(bf16)

    @pl.when(i < nu_ref[0])
    def _():
        x = x_ref[...]
        hmid = _silu(_dot(x, wg_b[...])) * _dot(x, wu_b[...])
        o_ref[...] = _dot(hmid.astype(bf16), wd_b[...]).astype(o_ref.dtype)

    @pl.when(i >= nu_ref[0])
    def _():
        o_ref[...] = jnp.zeros_like(o_ref)


def _experts(x_sorted, blk_e, n_used, wg, wu, wd, layer):
    n_rows, d = x_sorted.shape
    ff = wg.shape[3]
    bm = MOE_ROWS
    grid_spec = pltpu.PrefetchScalarGridSpec(
        num_scalar_prefetch=2, grid=(n_rows // bm,),
        in_specs=[pl.BlockSpec((bm, d), lambda i, be, nu: (i, 0)),
                  pl.BlockSpec((None, None, d, ff), lambda i, be, nu: (layer, be[i], 0, 0)),
                  pl.BlockSpec((None, None, d, ff), lambda i, be, nu: (layer, be[i], 0, 0)),
                  pl.BlockSpec((None, None, ff, d), lambda i, be, nu: (layer, be[i], 0, 0))],
        out_specs=pl.BlockSpec((bm, d), lambda i, be, nu: (i, 0)),
        scratch_shapes=[pltpu.VMEM((d, ff), bf16), pltpu.VMEM((d, ff), bf16), pltpu.VMEM((ff, d), bf16)])
    return pl.pallas_call(
        _expert_kernel, grid_spec=grid_spec,
        out_shape=jax.ShapeDtypeStruct((n_rows, d), bf16),
        compiler_params=_cp("arbitrary"), name="experts",
    )(blk_e, n_used, x_sorted, wg, wu, wd)


def _ffn_out_kernel(h_ref, r_ref, tw_ref, x_ref, mod_ref, g_ref, sg_ref, su_ref, sd_ref, o_ref):
    h = h_ref[...]
    mid = _silu(_dot(h, sg_ref[...])) * _dot(h, su_ref[...])
    f = _dot(mid.astype(bf16), sd_ref[...])
    for k in range(TOP_K):
        f = f + tw_ref[:, k:k + 1] * r_ref[k].astype(f32)
    o_ref[...] = x_ref[...] + mod_ref[5:6, :] * _rms(f, g_ref[...])


def _ffn_out(h2, routed, top_w, x, mods, g_post, sg, su, sd, rows):
    _, n, d = routed.shape
    tm = rows.tm
    ff = sg.shape[1]
    return pl.pallas_call(
        _ffn_out_kernel, grid=(rows.n,),
        in_specs=[pl.BlockSpec((tm, d), lambda i: (i, 0)),
                  pl.BlockSpec((TOP_K, tm, d), lambda i: (0, i, 0)),
                  pl.BlockSpec((tm, LANE), lambda i: (i, 0)),
                  pl.BlockSpec((tm, d), lambda i: (i, 0)),
                  pl.BlockSpec((None, 6, d), lambda i: (rows.cond(i), 0, 0)),
                  pl.BlockSpec((1, d), lambda i: (0, 0)),
                  pl.BlockSpec((d, ff), lambda i: (0, 0)),
                  pl.BlockSpec((d, ff), lambda i: (0, 0)),
                  pl.BlockSpec((ff, d), lambda i: (0, 0))],
        out_specs=pl.BlockSpec((tm, d), lambda i: (i, 0)),
        out_shape=jax.ShapeDtypeStruct((n, d), f32),
        compiler_params=_cp("parallel"), name="ffn_out",
    )(h2, routed, top_w, x, mods, g_post.reshape(1, d), sg, su, sd)


def _route_kernel(l_ref, b_ref, e_out, w_out):
    scores = _sigmoid(l_ref[...])
    sel = scores + b_ref[...]
    tm, ne = sel.shape
    per_group = ne // N_EXPERT_GROUPS
    lane = lax.broadcasted_iota(jnp.int32, (tm, ne), 1)
    lane_f = lane.astype(f32)
    grp = lane // per_group
    neg = -jnp.inf

    def first_max(v):
        m = jnp.max(v, axis=1, keepdims=True)
        return m, jnp.min(jnp.where(v == m, lane_f, float(ne)), axis=1, keepdims=True)

    gs = []
    for g in range(N_EXPERT_GROUPS):
        sg = jnp.where(grp == g, sel, neg)
        m1, i1 = first_max(sg)
        gs.append(m1 + jnp.max(jnp.where(lane_f == i1, neg, sg), axis=1, keepdims=True))
    chosen = [jnp.zeros((tm, 1), jnp.bool_)] * N_EXPERT_GROUPS
    for _ in range(TOPK_EXPERT_GROUPS):
        m = functools.reduce(jnp.maximum, gs)
        found = jnp.zeros((tm, 1), jnp.bool_)
        for g in range(N_EXPERT_GROUPS):
            hit = (gs[g] == m) & ~found
            chosen[g] = chosen[g] | hit
            found = found | hit
            gs[g] = jnp.where(hit, neg, gs[g])
    keep = jnp.zeros((tm, ne), jnp.bool_)
    for g in range(N_EXPERT_GROUPS):
        keep = keep | ((grp == g) & chosen[g])
    cur = jnp.where(keep, sel, neg)
    ids, wts = [], []
    for _ in range(TOP_K):
        _, idx = first_max(cur)
        hit = lane_f == idx
        ids.append(idx)
        wts.append(jnp.sum(jnp.where(hit, scores, 0.0), axis=1, keepdims=True))
        cur = jnp.where(hit, neg, cur)
    norm = ROUTE_SCALE / functools.reduce(lambda a, b: a + b, wts)
    out_lane = lax.broadcasted_iota(jnp.int32, (tm, LANE), 1)
    e = jnp.zeros((tm, LANE), f32)
    w = jnp.zeros((tm, LANE), f32)
    for k in range(TOP_K):
        e = jnp.where(out_lane == k, ids[k], e)
        w = jnp.where(out_lane == k, wts[k] * norm, w)
    e_out[...] = e.astype(jnp.int32)
    w_out[...] = w


def _route(logits, router_bias):
    t, ne = logits.shape
    tm = ROW_TILE
    e, w = pl.pallas_call(
        _route_kernel, grid=(t // tm,),
        in_specs=[pl.BlockSpec((tm, ne), lambda i: (i, 0)), pl.BlockSpec((1, ne), lambda i: (0, 0))],
        out_specs=[pl.BlockSpec((tm, LANE), lambda i: (i, 0)), pl.BlockSpec((tm, LANE), lambda i: (i, 0))],
        out_shape=[jax.ShapeDtypeStruct((t, LANE), jnp.int32), jax.ShapeDtypeStruct((t, LANE), f32)],
        compiler_params=_cp("parallel"), name="route",
    )(logits, router_bias.astype(f32).reshape(1, ne))
    return e[:, :TOP_K], w


def _moe(h2, logits, router_bias, wg, wu, wd, layer):
    t, d = logits.shape[0], h2.shape[1]
    top_e, w_all = _route(logits, router_bias)
    n_assign = t * TOP_K
    flat_e = top_e.reshape(n_assign)
    onehot = (flat_e[:, None] == jnp.arange(N_EXPERTS)[None, :]).astype(jnp.int32)
    csum = jnp.cumsum(onehot, axis=0)
    counts = csum[-1]
    rank = jnp.take_along_axis(csum, flat_e[:, None], axis=1)[:, 0] - 1
    padded = (counts + MOE_ROWS - 1) // MOE_ROWS * MOE_ROWS
    pad_end = jnp.cumsum(padded)
    dest = (pad_end - padded)[flat_e] + rank
    n_blocks = -(-n_assign // MOE_ROWS) + N_EXPERTS
    n_rows = n_blocks * MOE_ROWS
    row_tok = (jnp.arange(n_rows, dtype=jnp.int32) % t).at[dest].set(jnp.arange(n_assign, dtype=jnp.int32) // TOP_K)
    blk_e = jnp.minimum(jnp.searchsorted(pad_end, jnp.arange(n_blocks) * MOE_ROWS, side='right'),
                        N_EXPERTS - 1).astype(jnp.int32)
    n_used = (pad_end[-1] // MOE_ROWS).astype(jnp.int32).reshape(1)
    x_sorted = h2.at[row_tok].get(mode="promise_in_bounds")
    y_sorted = _experts(x_sorted, blk_e, n_used, wg, wu, wd, layer)
    dest_k = dest.reshape(t, TOP_K).T.reshape(n_assign)
    return y_sorted.at[dest_k].get(mode="promise_in_bounds").reshape(TOP_K, t, d), w_all


def _permute_w_in(w_in):
    cols = [w_in[:, _SRC[name][0]:_SRC[name][0] + _SRC[name][1]] for name in _P_ORDER]
    cols.append(jnp.zeros((w_in.shape[0], _P_WIDTH - _P_USED), w_in.dtype))
    return jnp.concatenate(cols, axis=1).astype(bf16)


def _permute_mla(w_uq, w_ukv):
    h = MLA_HEADS
    q = w_uq.reshape(-1, h, MLA_NOPE + MLA_ROPE)
    w_uq_p = jnp.concatenate([q[:, :, :MLA_NOPE].reshape(-1, h * MLA_NOPE), q[:, :, MLA_NOPE:].reshape(-1, h * MLA_ROPE)], 1)
    kv = w_ukv.reshape(-1, h, MLA_NOPE + MLA_V)
    w_ukv_p = jnp.concatenate([kv[:, :, :MLA_NOPE].reshape(-1, h * MLA_NOPE), kv[:, :, MLA_NOPE:].reshape(-1, h * MLA_V)], 1)
    return w_uq_p.astype(bf16), w_ukv_p.astype(bf16)


def _rope_tables(LC, LL):
    pos = jnp.arange(LL)
    n_freq = SWA_HEAD_DIM // 4
    inv_freq = ROPE_BASE ** (-jnp.arange(n_freq, dtype=f32) / n_freq)
    ang = jnp.concatenate([(pos // GRID_W).astype(f32)[:, None] * inv_freq,
                           (pos % GRID_W).astype(f32)[:, None] * inv_freq], axis=-1)
    cos, sin = jnp.cos(ang), jnp.sin(ang)
    cos = jnp.concatenate([jnp.ones((LC, cos.shape[1]), f32), cos], axis=0)
    sin = jnp.concatenate([jnp.zeros((LC, sin.shape[1]), f32), sin], axis=0)
    return jnp.tile(jnp.concatenate([cos, cos], 1), (1, 2)), jnp.tile(jnp.concatenate([-sin, sin], 1), (1, 2))


def _dir_lanes(v):
    out = jnp.zeros((2, LANE), f32)
    for d in range(2):
        out = out.at[d, _DT_LANE + d * SSD_HEADS:_DT_LANE + (d + 1) * SSD_HEADS].set(v[d].astype(f32))
    return out


def kernel(x, c, ctx, c_ctx, w_mod, b_mod, norm_pre_mix, norm_post_mix, norm_pre_ffn, norm_post_ffn, w_in, ssd_conv_w, ssd_conv_b, ssd_dt_bias, ssd_a_log, ssd_d, ssd_norm, swa_sink, hg_lb_logits, hg_norm, mla_q_norm, mla_kv_norm, mla_w_uq, mla_w_ukv, w_branch, w_out, router_w, router_bias, expert_w_gate, expert_w_up, expert_w_down, shared_w_gate, shared_w_up, shared_w_down):
    B, LL, d = x.shape
    LC = ctx.shape[1]
    S = LC + LL
    depth = w_mod.shape[0]
    assert LC % ROW_TILE == 0 and LL % ROW_TILE == 0 and LL >= 3 * WINDOW and d == D_MODEL

    n_cond = -(-(B + 1) // 8) * 8
    cond = jnp.zeros((n_cond, d), f32).at[:B].set(c).at[B].set(c_ctx)
    mods_all = _modulation(cond, w_mod, b_mod).reshape(depth, n_cond, 6, d)
    cos_s, sin_s = _rope_tables(LC, LL)
    sm = jax.nn.softmax(hg_lb_logits.astype(f32), axis=0)
    lower_bounds = jnp.cumsum(sm, axis=0) - sm[0]

    xs = jnp.concatenate([ctx, x], axis=1).reshape(B * S, d)
    for l in range(depth):
        last = l == depth - 1
        mods = mods_all[l]
        all_rows = _Rows(B, LC, LL, False)
        out_rows = _Rows(B, LC, LL, last)
        h = _norm_mod(xs, norm_pre_mix[l], mods, all_rows)
        p = _matmul(h, _permute_w_in(w_in[l]), 1024, _P_TN, bf16, "in_proj")
        p3 = p.reshape(B, S, _P_WIDTH)

        xbc3 = _ssd_conv(p3, ssd_conv_w[l], ssd_conv_b[l], LC, LL)
        y_ssd = _ssd_scan(xbc3, p3, _dir_lanes(ssd_dt_bias[l]), _dir_lanes(-jnp.exp(ssd_a_log[l].astype(f32))), LC, LL)
        d_in = SSD_HEADS * SSD_HEAD_DIM
        y_ssd = _ssd_finish(y_ssd.reshape(B * S, d_in), xbc3.reshape(B * S, -1), p,
                            jnp.repeat(ssd_d[l].astype(f32), SSD_HEAD_DIM).reshape(1, d_in), ssd_norm[l])
        y_swa = _swa(p3, swa_sink[l].astype(f32), cos_s, sin_s, LC, LL, last).reshape(B * S, -1)
        y_hg = _hgrn(p3, lower_bounds[l], hg_norm[l], LC, LL).reshape(B * S, -1)
        w_uq_p, w_ukv_p = _permute_mla(mla_w_uq[l], mla_w_ukv[l])
        qh, kh, vh = _mla_prep(p, mla_q_norm[l], mla_kv_norm[l], w_uq_p, w_ukv_p, cos_s, sin_s, S)
        y_mla = _mla_attn(qh, kh, vh, B, LC, LL, last)

        m = _merge((y_ssd, y_swa, y_hg, y_mla), p, w_branch[l].astype(bf16), out_rows)
        x_mid, h2, logits = _out_proj(m, w_out[l].astype(bf16), xs, mods, norm_post_mix[l], norm_pre_ffn[l],
                                      router_w[l], out_rows)
        routed, top_w = _moe(h2, logits, router_bias[l], expert_w_gate, expert_w_up, expert_w_down, l)
        xs = _ffn_out(h2, routed, top_w, x_mid, mods, norm_post_ffn[l], shared_w_gate[l].astype(bf16),
                      shared_w_up[l].astype(bf16), shared_w_down[l].astype(bf16), _Rows(B, LC, LL, last, tm=LANE))
    return xs.reshape(B, LL, d)
```
